```python
import math
import numpy as np
import jax
import jax.numpy as jnp
from jax import lax

D_MODEL = 1024
BATCH = 4
SEQ = 8192
DEPTH = 2

GRID_W = 64
CTX_LEN = 256
N_EVEN = (DEPTH + 1) // 2
N_ODD = DEPTH // 2
N_MOD = 9
DN_ALPHA = float((2 * DEPTH) ** 0.25)
DN_BETA = float((8 * DEPTH) ** -0.25)
LN_EPS = 1e-6
RMS_EPS = 1e-6
SUBLN_EPS = 1e-5
ROPE_BASE = 10000.0
Q_BLOCK = 128
NEG_INF = -1e30

D_FF = 2816

A_HEADS = 8
A_NOPE = 64
A_ROPE = 32
A_V = 64
A_Q_RANK = 256
A_KV_RANK = 128
A_SCALE = float((A_NOPE + A_ROPE) ** -0.5)
A_WIDTH = A_HEADS * A_V

B_HEADS = 4
B_HD = 64
B_SCALE = float(B_HD ** -0.5)
B_QK = B_HEADS * 2 * B_HD
B_WIDTH = B_HEADS * 2 * B_HD

C_GROUPS = 4
C_WINDOWS = (2, 4, 8, 16)
C_GW = 128
C_WIDTH = C_GROUPS * C_GW

D_HEADS = 8
D_HD = 64
D_SCALE = float(D_HD ** -0.5)
D_WIDTH = D_HEADS * D_HD
NA_ROWS = 8
NA_COLS = 16

EV_Q_COLS = A_Q_RANK + B_QK
EV_KV_COLS = A_KV_RANK + A_ROPE + B_QK + B_QK
EV_COLS = EV_Q_COLS + EV_KV_COLS
EV_MIX = A_WIDTH + B_WIDTH
OD_Q_COLS = C_WIDTH + D_WIDTH
OD_COLS = OD_Q_COLS + 2 * D_WIDTH
OD_MIX = C_WIDTH + D_WIDTH

kernel_name = 'hybrid_mla_diff_pool_natten_prefix_trunk'


def layer_norm(x, g, b):
    xf = x.astype(jnp.float32)
    mu = jnp.mean(xf, -1, keepdims=True)
    var = jnp.mean(jnp.square(xf - mu), -1, keepdims=True)
    y = (xf - mu) * lax.rsqrt(var + LN_EPS)
    return (y * g.astype(jnp.float32) + b.astype(jnp.float32)).astype(x.dtype)


def rms_norm(x, g, eps=RMS_EPS):
    xf = x.astype(jnp.float32)
    y = xf * lax.rsqrt(jnp.mean(jnp.square(xf), -1, keepdims=True) + eps)
    return (y * g.astype(jnp.float32)).astype(x.dtype)


def modulate(x, shift, scale):
    return x * (1 + scale) + shift


def post_norm(x, y, g, b):
    return layer_norm(DN_ALPHA * x + y, g, b)


def swiglu(h, w_gate, w_up, w_down):
    return (jax.nn.silu(h @ w_gate) * (h @ w_up)) @ w_down


def axial_rope_tables(n, rot_dim):
    t = jnp.arange(n, dtype=jnp.int32)
    row = (t // GRID_W).astype(jnp.float32)
    col = (t % GRID_W).astype(jnp.float32)
    quarter = rot_dim // 4
    inv = ROPE_BASE ** (-jnp.arange(quarter, dtype=jnp.float32) / quarter)
    ang_r = row[:, None] * inv[None, :]
    ang_c = col[:, None] * inv[None, :]
    ang = jnp.concatenate([ang_r, ang_r, ang_c, ang_c], -1)
    return jnp.cos(ang), jnp.sin(ang)


def maybe_rope(x, tables):
    if tables is None:
        return x
    cos, sin = tables
    if x.ndim == 4:
        cos, sin = cos[:, None, :], sin[:, None, :]
    xf = x.astype(jnp.float32)
    x1, x2, x3, x4 = jnp.split(xf, 4, axis=-1)
    rot = jnp.concatenate([-x2, x1, -x4, x3], -1)
    return (xf * cos + rot * sin).astype(x.dtype)


def softmax_f32(s, scale):
    return jax.nn.softmax(s.astype(jnp.float32) * scale, axis=-1)


def softmax_attend(q, k, v, scale):
    p = softmax_f32(jnp.einsum('bqhd,bkhd->bhqk', q, k), scale)
    return jnp.einsum('bhqk,bkhd->bqhd', p.astype(v.dtype), v)


def mla_attend(q_nope, q_pe, k_nope, k_pe, v):
    s = jnp.einsum('bqhd,bkhd->bhqk', q_nope, k_nope) + jnp.einsum('bqhr,bkr->bhqk', q_pe, k_pe)
    p = softmax_f32(s, A_SCALE)
    return jnp.einsum('bhqk,bkhd->bqhd', p.astype(v.dtype), v)


def diff_attend(q1, q2, k1, k2, v, lam, lam_init, g_sub):
    p1 = softmax_f32(jnp.einsum('bqhd,bkhd->bhqk', q1, k1), B_SCALE)
    p2 = softmax_f32(jnp.einsum('bqhd,bkhd->bhqk', q2, k2), B_SCALE)
    o = jnp.einsum('bhqk,bkhd->bqhd', (p1 - lam * p2).astype(v.dtype), v)
    return rms_norm(o, g_sub, SUBLN_EPS) * (1.0 - lam_init)


def sweep_query_blocks(fn, *qs):
    b, n = qs[0].shape[:2]
    nb = n // Q_BLOCK
    blocks = tuple(jnp.moveaxis(q.reshape((b, nb, Q_BLOCK) + q.shape[2:]), 1, 0) for q in qs)
    out = lax.map(lambda blk: fn(*blk), blocks)
    out = jnp.moveaxis(out, 0, 1)
    return out.reshape((b, n) + out.shape[3:])


def even_queries(qp, g_qlat, w_uq, rope_a, rope_b):
    b, n, _ = qp.shape
    q = (rms_norm(qp[..., :A_Q_RANK], g_qlat) @ w_uq).reshape(b, n, A_HEADS, A_NOPE + A_ROPE)
    bq = qp[..., A_Q_RANK:].reshape(b, n, B_HEADS, 2, B_HD)
    return (q[..., :A_NOPE], maybe_rope(q[..., A_NOPE:], rope_a),
            maybe_rope(bq[..., 0, :], rope_b), maybe_rope(bq[..., 1, :], rope_b))


def even_keys(kp, g_kvlat, w_ukv, rope_a, rope_b):
    b, n, _ = kp.shape
    kv = (rms_norm(kp[..., :A_KV_RANK], g_kvlat) @ w_ukv).reshape(b, n, A_HEADS, A_NOPE + A_V)
    k_pe = maybe_rope(kp[..., A_KV_RANK:A_KV_RANK + A_ROPE], rope_a)
    o = A_KV_RANK + A_ROPE
    bk = kp[..., o:o + B_QK].reshape(b, n, B_HEADS, 2, B_HD)
    v_b = kp[..., o + B_QK:].reshape(b, n, B_HEADS, 2 * B_HD)
    return (kv[..., :A_NOPE], k_pe, kv[..., A_NOPE:],
            maybe_rope(bk[..., 0, :], rope_b), maybe_rope(bk[..., 1, :], rope_b), v_b)


def even_mixer(h, hc, w_in, w_out, g_qlat, g_kvlat, w_uq, w_ukv, lam_vec, g_sub, lam_init, ctx_out):
    b, n, _ = h.shape
    m = hc.shape[1]
    rope_a = axial_rope_tables(n, A_ROPE)
    rope_b = axial_rope_tables(n, B_HD)
    lv = lam_vec.astype(jnp.float32)
    lam = jnp.exp(jnp.sum(lv[0] * lv[1])) - jnp.exp(jnp.sum(lv[2] * lv[3])) + lam_init
    p = h @ w_in
    pc = hc @ (w_in if ctx_out else w_in[:, EV_Q_COLS:])
    keys_ctx = even_keys(pc[..., -EV_KV_COLS:], g_kvlat, w_ukv, None, None)
    keys_lat = even_keys(p[..., EV_Q_COLS:], g_kvlat, w_ukv, rope_a, rope_b)
    kn, kpe, va, k1, k2, vb = [jnp.concatenate([kl, kc], axis=1) for kl, kc in zip(keys_lat, keys_ctx)]

    def block(qn, qpe, q1, q2):
        bq = qn.shape[1]
        oa = mla_attend(qn, qpe, kn, kpe, va).reshape(b, bq, A_WIDTH)
        ob = diff_attend(q1, q2, k1, k2, vb, lam, lam_init, g_sub).reshape(b, bq, B_WIDTH)
        return jnp.concatenate([oa, ob], -1)

    y = sweep_query_blocks(block, *even_queries(p[..., :EV_Q_COLS], g_qlat, w_uq, rope_a, rope_b)) @ w_out
    if not ctx_out:
        return y, None
    kn_c, kpe_c, va_c, k1_c, k2_c, vb_c = keys_ctx
    qn_c, qpe_c, q1_c, q2_c = even_queries(pc[..., :EV_Q_COLS], g_qlat, w_uq, None, None)
    oa_c = mla_attend(qn_c, qpe_c, kn_c, kpe_c, va_c).reshape(b, m, A_WIDTH)
    ob_c = diff_attend(q1_c, q2_c, k1_c, k2_c, vb_c, lam, lam_init, g_sub).reshape(b, m, B_WIDTH)
    yc = jnp.concatenate([oa_c, ob_c], -1) @ w_out
    return y, yc


def window_mean(u, w):
    n = u.shape[1]
    left = w // 2
    right = w - 1 - left
    t = np.arange(n)
    lo = np.clip(t - left, 0, n)
    hi = np.clip(t + right + 1, 0, n)
    uf = u.astype(jnp.float32)
    csum = jnp.concatenate([jnp.zeros_like(uf[:, :1]), jnp.cumsum(uf, axis=1)], axis=1)
    cnt = jnp.asarray(hi - lo, jnp.float32)
    return ((csum[:, hi] - csum[:, lo]) / cnt[None, :, None]).astype(u.dtype)


def multiscale_pool(u, w_pool, pool_scale):
    b, n, _ = u.shape
    ug = u.reshape(b, n, C_GROUPS, C_GW)
    pooled = jnp.stack([window_mean(ug[:, :, g], w) for g, w in enumerate(C_WINDOWS)], axis=2) - ug
    return jnp.einsum('bngc,gcd->bngd', pooled, w_pool).reshape(b, n, C_WIDTH) * pool_scale


def neighbourhood_attend(q, k, v, k_ctx, v_ctx, rpb):
    b, n, h, d = q.shape
    rows = n // GRID_W
    wr = min(NA_ROWS, rows)
    rs = np.clip(np.arange(rows) - wr // 2, 0, rows - wr)
    j = np.arange(GRID_W)
    cs = np.clip(j - NA_COLS // 2, 0, GRID_W - NA_COLS)
    col_valid = (j[None, :] >= cs[:, None]) & (j[None, :] < cs[:, None] + NA_COLS)
    col_idx = np.clip(j[None, :] - j[:, None] + NA_COLS - 1, 0, 2 * NA_COLS - 2)
    q_rows = jnp.moveaxis(q.reshape(b, rows, GRID_W, h, d), 1, 0)
    k_grid = k.reshape(b, rows, GRID_W, h, d)
    v_grid = v.reshape(b, rows, GRID_W, h, d)
    nk = wr * GRID_W

    def row_fn(args):
        q_r, r, r0 = args
        k_blk = lax.dynamic_slice_in_dim(k_grid, r0, wr, axis=1).reshape(b, nk, h, d)
        v_blk = lax.dynamic_slice_in_dim(v_grid, r0, wr, axis=1).reshape(b, nk, h, d)
        row_off = r0 + jnp.arange(wr, dtype=jnp.int32) - r + NA_ROWS - 1
        bias = rpb[:, row_off[:, None, None], col_idx[None]].astype(jnp.float32)
        bias = jnp.where(col_valid[None, None], bias, NEG_INF)
        bias = bias.transpose(0, 2, 1, 3).reshape(h, GRID_W, nk)
        s_nb = jnp.einsum('bqhd,bkhd->bhqk', q_r, k_blk).astype(jnp.float32) * D_SCALE + bias
        s_cx = jnp.einsum('bqhd,bkhd->bhqk', q_r, k_ctx).astype(jnp.float32) * D_SCALE
        p = jax.nn.softmax(jnp.concatenate([s_nb, s_cx], -1), axis=-1).astype(v.dtype)
        return (jnp.einsum('bhqk,bkhd->bqhd', p[..., :nk], v_blk)
                + jnp.einsum('bhqk,bkhd->bqhd', p[..., nk:], v_ctx))

    out = lax.map(row_fn, (q_rows, jnp.arange(rows, dtype=jnp.int32), jnp.asarray(rs, jnp.int32)))
    return jnp.moveaxis(out, 0, 1).reshape(b, n, h, d)


def odd_mixer(h, hc, w_in, w_out, w_pool, pool_scale, rpb, ctx_out):
    b, n, _ = h.shape
    m = hc.shape[1]
    p = h @ w_in
    pc = hc @ (w_in if ctx_out else w_in[:, OD_Q_COLS:])
    q = p[..., C_WIDTH:OD_Q_COLS].reshape(b, n, D_HEADS, D_HD)
    k = p[..., OD_Q_COLS:OD_Q_COLS + D_WIDTH].reshape(b, n, D_HEADS, D_HD)
    v = p[..., OD_Q_COLS + D_WIDTH:].reshape(b, n, D_HEADS, D_HD)
    k_c = pc[..., -2 * D_WIDTH:-D_WIDTH].reshape(b, m, D_HEADS, D_HD)
    v_c = pc[..., -D_WIDTH:].reshape(b, m, D_HEADS, D_HD)
    y_pool = multiscale_pool(p[..., :C_WIDTH], w_pool, pool_scale)
    y_na = neighbourhood_attend(q, k, v, k_c, v_c, rpb).reshape(b, n, D_WIDTH)
    y = jnp.concatenate([y_pool, y_na], -1) @ w_out
    if not ctx_out:
        return y, None
    q_c = pc[..., C_WIDTH:OD_Q_COLS].reshape(b, m, D_HEADS, D_HD)
    yc_pool = multiscale_pool(pc[..., :C_WIDTH], w_pool, pool_scale)
    yc_att = softmax_attend(q_c, k_c, v_c, D_SCALE).reshape(b, m, D_WIDTH)
    yc = jnp.concatenate([yc_pool, yc_att], -1) @ w_out
    return y, yc


def setup_inputs(seed: int = 0) -> dict:
    key = jax.random.key(seed)
    ks = jax.random.split(key, 26)

    def nrm(k, shape, s):
        return jax.random.normal(k, shape, jnp.float32) * s

    return {
        'x': nrm(ks[0], (BATCH, SEQ, D_MODEL), 1.0),
        'c': nrm(ks[1], (BATCH, D_MODEL), 1.0),
        'ctx': nrm(ks[2], (BATCH, CTX_LEN, D_MODEL), 1.0),
        'c_ctx': nrm(ks[3], (D_MODEL,), 1.0),
        'ada_w': nrm(ks[4], (DEPTH, D_MODEL, N_MOD * D_MODEL), 0.5 * D_MODEL ** -0.5),
        'ada_b': nrm(ks[5], (DEPTH, N_MOD * D_MODEL), 0.02),
        'ln_g': 1.0 + nrm(ks[6], (DEPTH, 3, D_MODEL), 0.02),
        'ln_b': nrm(ks[7], (DEPTH, 3, D_MODEL), 0.02),
        'ffn_w_gate': nrm(ks[8], (DEPTH, 2, D_MODEL, D_FF), D_MODEL ** -0.5),
        'ffn_w_up': nrm(ks[9], (DEPTH, 2, D_MODEL, D_FF), D_MODEL ** -0.5),
        'ffn_w_down': nrm(ks[10], (DEPTH, 2, D_FF, D_MODEL), DN_BETA * D_FF ** -0.5),
        'ev_w_in': nrm(ks[11], (N_EVEN, D_MODEL, EV_COLS), D_MODEL ** -0.5),
        'ev_w_out': nrm(ks[12], (N_EVEN, EV_MIX, D_MODEL), DN_BETA * EV_MIX ** -0.5),
        'ev_g_qlat': 1.0 + nrm(ks[13], (N_EVEN, A_Q_RANK), 0.02),
        'ev_g_kvlat': 1.0 + nrm(ks[14], (N_EVEN, A_KV_RANK), 0.02),
        'ev_w_uq': nrm(ks[15], (N_EVEN, A_Q_RANK, A_HEADS * (A_NOPE + A_ROPE)), A_Q_RANK ** -0.5),
        'ev_w_ukv': nrm(ks[16], (N_EVEN, A_KV_RANK, A_HEADS * (A_NOPE + A_V)), A_KV_RANK ** -0.5),
        'ev_lam': nrm(ks[17], (N_EVEN, 4, B_HD), 0.1),
        'ev_g_sub': 1.0 + nrm(ks[18], (N_EVEN, 2 * B_HD), 0.02),
        'od_w_in': nrm(ks[19], (N_ODD, D_MODEL, OD_COLS), D_MODEL ** -0.5),
        'od_w_out': nrm(ks[20], (N_ODD, OD_MIX, D_MODEL), DN_BETA * OD_MIX ** -0.5),
        'od_w_pool': nrm(ks[21], (N_ODD, C_GROUPS, C_GW, C_GW), C_GW ** -0.5),
        'od_pool_scale': 1.0 + nrm(ks[22], (N_ODD, C_WIDTH), 0.02),
        'od_rpb': nrm(ks[23], (N_ODD, D_HEADS, 2 * NA_ROWS - 1, 2 * NA_COLS - 1), 0.02),
    }


def reference(x, c, ctx, c_ctx, ada_w, ada_b, ln_g, ln_b, ffn_w_gate, ffn_w_up, ffn_w_down,
              ev_w_in, ev_w_out, ev_g_qlat, ev_g_kvlat, ev_w_uq, ev_w_ukv, ev_lam, ev_g_sub,
              od_w_in, od_w_out, od_w_pool, od_pool_scale, od_rpb):
    bsz = x.shape[0]
    sc = jax.nn.silu(c)
    sc_ctx = jax.nn.silu(c_ctx)
    h, hc = x, ctx
    for l in range(DEPTH):
        last = l == DEPTH - 1
        mod = (sc @ ada_w[l] + ada_b[l]).reshape(bsz, N_MOD, 1, D_MODEL)
        mod_c = (sc_ctx @ ada_w[l] + ada_b[l]).reshape(N_MOD, D_MODEL)

        h = post_norm(h, 0.5 * mod[:, 2] * swiglu(modulate(h, mod[:, 0], mod[:, 1]),
                                                   ffn_w_gate[l, 0], ffn_w_up[l, 0], ffn_w_down[l, 0]),
                      ln_g[l, 0], ln_b[l, 0])
        hc = post_norm(hc, 0.5 * mod_c[2] * swiglu(modulate(hc, mod_c[0], mod_c[1]),
                                                    ffn_w_gate[l, 0], ffn_w_up[l, 0], ffn_w_down[l, 0]),
                       ln_g[l, 0], ln_b[l, 0])

        hm = modulate(h, mod[:, 3], mod[:, 4])
        hcm = modulate(hc, mod_c[3], mod_c[4])
        if l % 2 == 0:
            e = l // 2
            lam_init = 0.8 - 0.6 * math.exp(-0.3 * l)
            y, yc = even_mixer(hm, hcm, ev_w_in[e], ev_w_out[e], ev_g_qlat[e], ev_g_kvlat[e],
                               ev_w_uq[e], ev_w_ukv[e], ev_lam[e], ev_g_sub[e], lam_init, not last)
        else:
            o = l // 2
            y, yc = odd_mixer(hm, hcm, od_w_in[o], od_w_out[o], od_w_pool[o], od_pool_scale[o],
                              od_rpb[o], not last)
        h = post_norm(h, mod[:, 5] * y, ln_g[l, 1], ln_b[l, 1])

        h = post_norm(h, 0.5 * mod[:, 8] * swiglu(modulate(h, mod[:, 6], mod[:, 7]),
                                                   ffn_w_gate[l, 1], ffn_w_up[l, 1], ffn_w_down[l, 1]),
                      ln_g[l, 2], ln_b[l, 2])
        if not last:
            hc = post_norm(hc, mod_c[5] * yc, ln_g[l, 1], ln_b[l, 1])
            hc = post_norm(hc, 0.5 * mod_c[8] * swiglu(modulate(hc, mod_c[6], mod_c[7]),
                                                        ffn_w_gate[l, 1], ffn_w_up[l, 1], ffn_w_down[l, 1]),
                           ln_g[l, 2], ln_b[l, 2])
    return h
```

```python
import functools
import math

import numpy as np
import jax
import jax.numpy as jnp
from jax import lax
from jax.experimental import pallas as pl
from jax.experimental.pallas import tpu as pltpu

F32 = jnp.float32
BF16 = jnp.bfloat16

D_MODEL = 1024
DEPTH = 2
GRID_W = 64
N_MOD = 9
DN_ALPHA = float((2 * DEPTH) ** 0.25)
LN_EPS = 1e-6
RMS_EPS = 1e-6
SUBLN_EPS = 1e-5
ROPE_BASE = 10000.0
NEG_INF = -1e30
D_FF = 2816

A_HEADS = 8
A_NOPE = 64
A_ROPE = 32
A_V = 64
A_Q_RANK = 256
A_KV_RANK = 128
A_SCALE = float((A_NOPE + A_ROPE) ** -0.5)
A_QK = A_KV_RANK + A_ROPE

B_HEADS = 4
B_HD = 64
B_SCALE = float(B_HD ** -0.5)
B_QK = B_HEADS * 2 * B_HD

C_GROUPS = 4
C_WINDOWS = (2, 4, 8, 16)
C_GW = 128
C_WIDTH = C_GROUPS * C_GW

D_HEADS = 8
D_HD = 64
D_SCALE = float(D_HD ** -0.5)
D_WIDTH = D_HEADS * D_HD
NA_ROWS = 8
NA_COLS = 16

LANES = 128
TM = 512
TQ = 256
TK = 512
VB = 256
FF_CHUNK = 1408
POOL_HALO = 8
NA_KEY_ROWS = 12
NA_PATTERNS = 3
VMEM_LIMIT = 56 * 1024 * 1024


def _cparams(n_axes):
    return pltpu.CompilerParams(dimension_semantics=("arbitrary",) * n_axes,
                                vmem_limit_bytes=VMEM_LIMIT)


def _resident(shape, index_map):
    return pl.BlockSpec(shape, index_map, pipeline_mode=pl.Buffered(1))


def _layer_norm(z, g, b):
    mu = jnp.mean(z, axis=-1, keepdims=True)
    zc = z - mu
    var = jnp.mean(zc * zc, axis=-1, keepdims=True)
    return zc * lax.rsqrt(var + LN_EPS) * g + b


def _rms_norm(x, g, eps):
    return x * lax.rsqrt(jnp.mean(x * x, axis=-1, keepdims=True) + eps) * g


def _dot(a, b):
    return jnp.dot(a, b, preferred_element_type=F32)


def _dot_nt(a, b):
    return lax.dot_general(a, b, (((1,), (1,)), ((), ())), preferred_element_type=F32)


def _dot_tn(a, b):
    return lax.dot_general(a, b, (((0,), (0,)), ((), ())), preferred_element_type=F32)


def _mod_kernel(c_ref, w_ref, b_ref, o_ref):
    x = c_ref[...]
    s = (x * jax.nn.sigmoid(x)).astype(BF16)
    o_ref[0] = _dot(s, w_ref[0].astype(BF16)) + b_ref[0]


def _modulation(c_all, ada_w, ada_b):
    depth = ada_w.shape[0]
    return pl.pallas_call(
        _mod_kernel,
        grid=(depth, N_MOD),
        in_specs=[pl.BlockSpec((8, D_MODEL), lambda l, j: (0, 0)),
                  pl.BlockSpec((1, D_MODEL, D_MODEL), lambda l, j: (l, 0, j)),
                  pl.BlockSpec((1, 1, D_MODEL), lambda l, j: (l, 0, j))],
        out_specs=pl.BlockSpec((1, 8, D_MODEL), lambda l, j: (l, 0, j)),
        out_shape=jax.ShapeDtypeStruct((depth, 8, N_MOD * D_MODEL), F32),
        compiler_params=_cparams(2),
        name="modulation",
    )(c_all, ada_w, ada_b.reshape(depth, 1, N_MOD * D_MODEL))


def _ffn_kernel(h_ref, mod_ref, wg_ref, wu_ref, wd_ref, g_ref, b_ref, o_ref):
    h = h_ref[...]
    shift, scale, gate = mod_ref[0, 0:1, :], mod_ref[0, 1:2, :], mod_ref[0, 2:3, :]
    hm = (h * (1.0 + scale) + shift).astype(BF16)
    y = jnp.zeros((TM, D_MODEL), F32)
    for c in range(D_FF // FF_CHUNK):
        cols = slice(c * FF_CHUNK, (c + 1) * FF_CHUNK)
        gt = _dot(hm, wg_ref[:, cols])
        up = _dot(hm, wu_ref[:, cols])
        act = (gt * jax.nn.sigmoid(gt) * up).astype(BF16)
        y = y + _dot(act, wd_ref[cols, :])
    o_ref[...] = _layer_norm(DN_ALPHA * h + (0.5 * gate) * y, g_ref[...], b_ref[...])


def _ffn(h, mod3, wg, wu, wd, g, b, n_tiles, mod_index):
    return pl.pallas_call(
        _ffn_kernel,
        grid=(n_tiles,),
        in_specs=[pl.BlockSpec((TM, D_MODEL), lambda i: (i, 0)),
                  pl.BlockSpec((1, 3, D_MODEL), lambda i: (mod_index(i), 0, 0)),
                  _resident((D_MODEL, D_FF), lambda i: (0, 0)),
                  _resident((D_MODEL, D_FF), lambda i: (0, 0)),
                  _resident((D_FF, D_MODEL), lambda i: (0, 0)),
                  pl.BlockSpec((1, D_MODEL), lambda i: (0, 0)),
                  pl.BlockSpec((1, D_MODEL), lambda i: (0, 0))],
        out_specs=pl.BlockSpec((TM, D_MODEL), lambda i: (i, 0)),
        out_shape=jax.ShapeDtypeStruct((n_tiles * TM, D_MODEL), F32),
        compiler_params=_cparams(1),
        name="ffn_postnorm",
    )(h, mod3, wg, wu, wd, g, b)


def _rope(x, cos, sin_signed, quarter):
    lane = lax.broadcasted_iota(jnp.int32, x.shape, 1)
    first = (lane % (2 * quarter)) < quarter
    rot = jnp.where(first, pltpu.roll(x, LANES - quarter, 1), pltpu.roll(x, quarter, 1))
    return x * cos + rot * sin_signed


EV_QLAT = 0
EV_BQ = 256
EV_KVLAT = 768
EV_KPE = 896
EV_BK = 1024
EV_BV = 1536
EV_PCOLS = 2048


def _even_proj_kernel(h_ref, mod_ref, win_ref, gq_ref, gkv_ref, wuq_ref, wabs_ref,
                      cosa_ref, sina_ref, cosb_ref, sinb_ref,
                      qa_ref, ka_ref, ct_ref, qb_ref, kb_ref, vbt_ref):
    h = h_ref[...]
    hm = (h * (1.0 + mod_ref[0, 1:2, :]) + mod_ref[0, 0:1, :]).astype(BF16)
    p = _dot(hm, win_ref[...])
    cosa, sina, cosb, sinb = cosa_ref[...], sina_ref[...], cosb_ref[...], sinb_ref[...]

    qn = _rms_norm(p[:, EV_QLAT:EV_QLAT + A_Q_RANK], gq_ref[...], RMS_EPS).astype(BF16)
    q = _dot(qn, wuq_ref[...])
    n_nope = A_HEADS * A_NOPE
    qabs = _dot(q[:, :n_nope].astype(BF16), wabs_ref[...])
    qpe = [_rope(q[:, n_nope + LANES * s:n_nope + LANES * (s + 1)], cosa, sina, A_ROPE // 4) for s in range(2)]
    per_slab = LANES // A_ROPE
    for hd in range(A_HEADS):
        qa_ref[hd, :, 0:A_KV_RANK] = (qabs[:, A_KV_RANK * hd:A_KV_RANK * (hd + 1)] * A_SCALE).astype(BF16)
        pe = qpe[hd // per_slab][:, A_ROPE * (hd % per_slab):A_ROPE * (hd % per_slab + 1)]
        qa_ref[hd, :, A_KV_RANK:A_QK] = (pe * A_SCALE).astype(BF16)

    cn = _rms_norm(p[:, EV_KVLAT:EV_KVLAT + A_KV_RANK], gkv_ref[...], RMS_EPS)
    kpe = _rope(p[:, EV_KPE:EV_KPE + LANES], cosa, sina, A_ROPE // 4)
    ka_ref[:, 0:A_KV_RANK] = cn.astype(BF16)
    ka_ref[:, A_KV_RANK:A_QK] = kpe[:, 0:A_ROPE].astype(BF16)
    cnt = cn.T.astype(BF16)
    for s in range(TM // VB):
        ct_ref[s] = cnt[:, VB * s:VB * (s + 1)]

    for hd in range(B_HEADS):
        sl = slice(LANES * hd, LANES * (hd + 1))
        qb_ref[hd] = (_rope(p[:, EV_BQ:EV_BQ + B_QK][:, sl], cosb, sinb, B_HD // 4) * B_SCALE).astype(BF16)
        kb_ref[hd] = _rope(p[:, EV_BK:EV_BK + B_QK][:, sl], cosb, sinb, B_HD // 4).astype(BF16)
        vt = p[:, EV_BV:EV_BV + B_QK][:, sl].T.astype(BF16)
        for s in range(TM // VB):
            vbt_ref[hd, s] = vt[:, VB * s:VB * (s + 1)]


def _even_proj(h, mod3, win, gq, gkv, wuq, wabs, tabs, n_tiles, mod_index, tab_index):
    t_all = n_tiles * TM
    row = lambda i: (i, 0)
    tab = lambda i: (tab_index(i), 0)
    const = lambda i: (0, 0)
    return pl.pallas_call(
        _even_proj_kernel,
        grid=(n_tiles,),
        in_specs=[pl.BlockSpec((TM, D_MODEL), row),
                  pl.BlockSpec((1, 3, D_MODEL), lambda i: (mod_index(i), 0, 0)),
                  _resident((D_MODEL, EV_PCOLS), const),
                  pl.BlockSpec((1, A_Q_RANK), const),
                  pl.BlockSpec((1, A_KV_RANK), const),
                  _resident((A_Q_RANK, A_HEADS * (A_NOPE + A_ROPE)), const),
                  _resident((A_HEADS * A_NOPE, A_HEADS * A_KV_RANK), const),
                  pl.BlockSpec((TM, LANES), tab), pl.BlockSpec((TM, LANES), tab),
                  pl.BlockSpec((TM, LANES), tab), pl.BlockSpec((TM, LANES), tab)],
        out_specs=[pl.BlockSpec((A_HEADS, TM, A_QK), lambda i: (0, i, 0)),
                   pl.BlockSpec((TM, A_QK), row),
                   pl.BlockSpec((TM // VB, A_KV_RANK, VB), lambda i: (i, 0, 0)),
                   pl.BlockSpec((B_HEADS, TM, LANES), lambda i: (0, i, 0)),
                   pl.BlockSpec((B_HEADS, TM, LANES), lambda i: (0, i, 0)),
                   pl.BlockSpec((B_HEADS, TM // VB, LANES, VB), lambda i: (0, i, 0, 0))],
        out_shape=[jax.ShapeDtypeStruct((A_HEADS, t_all, A_QK), BF16),
                   jax.ShapeDtypeStruct((t_all, A_QK), BF16),
                   jax.ShapeDtypeStruct((t_all // VB, A_KV_RANK, VB), BF16),
                   jax.ShapeDtypeStruct((B_HEADS, t_all, LANES), BF16),
                   jax.ShapeDtypeStruct((B_HEADS, t_all, LANES), BF16),
                   jax.ShapeDtypeStruct((B_HEADS, t_all // VB, LANES, VB), BF16)],
        compiler_params=_cparams(1),
        name="even_proj",
    )(h, mod3, win, gq, gkv, wuq, wabs, *tabs)


def _softmax_reset(m_ref, l_ref, acc_ref):
    m_ref[...] = jnp.full(m_ref.shape, NEG_INF, F32)
    l_ref[...] = jnp.zeros(l_ref.shape, F32)
    acc_ref[...] = jnp.zeros(acc_ref.shape, F32)


def _softmax_step(s, vt_blocks, m_ref, l_ref, acc_ref):
    m_old = m_ref[...]
    m_new = jnp.maximum(m_old, jnp.max(s, axis=0, keepdims=True))
    alpha = jnp.exp(m_old - m_new)
    p = jnp.exp(s - m_new)
    l_ref[...] = alpha * l_ref[...] + jnp.sum(p, axis=0, keepdims=True)
    m_ref[...] = m_new
    pb = p.astype(BF16)
    pv = _dot(vt_blocks[0], pb[0:VB])
    for i in range(1, len(vt_blocks)):
        pv = pv + _dot(vt_blocks[i], pb[VB * i:VB * (i + 1)])
    acc_ref[...] = alpha * acc_ref[...] + pv


def _block_diag_queries(q12):
    lane = lax.broadcasted_iota(jnp.int32, q12.shape, 1)
    zero = jnp.zeros_like(q12)
    half = LANES // 2
    return jnp.concatenate([jnp.where(lane < half, q12, zero), jnp.where(lane >= half, q12, zero)], axis=0)


def _mla_kernel(qa_ref, kl_ref, kc_ref, ctl_ref, ctc_ref, wuvt_ref, o_ref, m_ref, l_ref, acc_ref,
                *, n_lat_steps, n_qb_lat):
    n_steps = jnp.where(pl.program_id(1) < n_qb_lat, n_lat_steps, 0)

    def head(hd, carry):
        q = qa_ref[hd]
        _softmax_reset(m_ref, l_ref, acc_ref)

        def step(j, c):
            k = kl_ref[pl.ds(pl.multiple_of(j * TK, TK), TK), :]
            blocks = [ctl_ref[(TK // VB) * j + i] for i in range(TK // VB)]
            _softmax_step(_dot_nt(k, q), blocks, m_ref, l_ref, acc_ref)
            return c

        lax.fori_loop(0, n_steps, step, 0)
        _softmax_step(_dot_nt(kc_ref[...], q), [ctc_ref[0]], m_ref, l_ref, acc_ref)
        o_lat = (acc_ref[...] * (1.0 / l_ref[...])).astype(BF16)
        o = _dot(wuvt_ref[hd], o_lat)
        o_ref[pl.ds(pl.multiple_of(hd * A_V, A_V), A_V), :] = o.astype(BF16)
        return carry

    lax.fori_loop(0, A_HEADS, head, 0)


def _mla(qa, ka, ct, wuvt, bsz, n, m):
    t_lat = bsz * n
    t_all = t_lat + bsz * m
    nqb = n // TQ
    qblk = lambda b, j: jnp.where(j < nqb, b * nqb + j, t_lat // TQ + b)
    kern = functools.partial(_mla_kernel, n_lat_steps=n // TK, n_qb_lat=nqb)
    return pl.pallas_call(
        kern,
        grid=(bsz, nqb + 1),
        in_specs=[pl.BlockSpec((A_HEADS, TQ, A_QK), lambda b, j: (0, qblk(b, j), 0)),
                  _resident((n, A_QK), lambda b, j: (b, 0)),
                  pl.BlockSpec((m, A_QK), lambda b, j: (t_lat // m + b, 0)),
                  _resident((n // VB, A_KV_RANK, VB), lambda b, j: (b, 0, 0)),
                  pl.BlockSpec((m // VB, A_KV_RANK, VB), lambda b, j: (t_lat // m + b, 0, 0)),
                  pl.BlockSpec((A_HEADS, A_V, A_KV_RANK), lambda b, j: (0, 0, 0))],
        out_specs=pl.BlockSpec((A_HEADS * A_V, TQ), lambda b, j: (0, qblk(b, j))),
        out_shape=jax.ShapeDtypeStruct((A_HEADS * A_V, t_all), BF16),
        scratch_shapes=[pltpu.VMEM((1, TQ), F32), pltpu.VMEM((1, TQ), F32),
                        pltpu.VMEM((A_KV_RANK, TQ), F32)],
        compiler_params=_cparams(2),
        name="mla_attention",
    )(qa, ka, ka, ct, ct, wuvt)


def _diff_kernel(qb_ref, kl_ref, kc_ref, vl_ref, vc_ref, lam_ref, gsub_ref, o_ref, m_ref, l_ref, acc_ref,
                 *, n_lat_steps, n_qb_lat, lam_init):
    n_steps = jnp.where(pl.program_id(1) < n_qb_lat, n_lat_steps, 0)
    lv = lam_ref[...]
    lam = (jnp.exp(jnp.sum(lv[0:1] * lv[1:2], axis=1, keepdims=True))
           - jnp.exp(jnp.sum(lv[2:3] * lv[3:4], axis=1, keepdims=True)) + lam_init)

    def head(hd, carry):
        qbd = _block_diag_queries(qb_ref[hd])
        _softmax_reset(m_ref, l_ref, acc_ref)

        def step(j, c):
            k = kl_ref[hd, pl.ds(pl.multiple_of(j * TK, TK), TK), :]
            blocks = [vl_ref[hd, (TK // VB) * j + i] for i in range(TK // VB)]
            _softmax_step(_dot_nt(k, qbd), blocks, m_ref, l_ref, acc_ref)
            return c

        lax.fori_loop(0, n_steps, step, 0)
        _softmax_step(_dot_nt(kc_ref[hd], qbd), [vc_ref[hd, 0]], m_ref, l_ref, acc_ref)
        on = acc_ref[...] * (1.0 / l_ref[...])
        o = on[:, 0:TQ] - lam * on[:, TQ:2 * TQ]
        o = o * lax.rsqrt(jnp.mean(o * o, axis=0, keepdims=True) + SUBLN_EPS) * gsub_ref[...]
        o_ref[pl.ds(pl.multiple_of(hd * LANES, LANES), LANES), :] = (o * (1.0 - lam_init)).astype(BF16)
        return carry

    lax.fori_loop(0, B_HEADS, head, 0)


def _diff(qb, kb, vbt, lam_vec, gsub_col, lam_init, bsz, n, m):
    t_lat = bsz * n
    t_all = t_lat + bsz * m
    nqb = n // TQ
    qblk = lambda b, j: jnp.where(j < nqb, b * nqb + j, t_lat // TQ + b)
    kern = functools.partial(_diff_kernel, n_lat_steps=n // TK, n_qb_lat=nqb, lam_init=lam_init)
    return pl.pallas_call(
        kern,
        grid=(bsz, nqb + 1),
        in_specs=[pl.BlockSpec((B_HEADS, TQ, LANES), lambda b, j: (0, qblk(b, j), 0)),
                  _resident((B_HEADS, n, LANES), lambda b, j: (0, b, 0)),
                  pl.BlockSpec((B_HEADS, m, LANES), lambda b, j: (0, t_lat // m + b, 0)),
                  _resident((B_HEADS, n // VB, LANES, VB), lambda b, j: (0, b, 0, 0)),
                  pl.BlockSpec((B_HEADS, m // VB, LANES, VB), lambda b, j: (0, t_lat // m + b, 0, 0)),
                  pl.BlockSpec((4, B_HD), lambda b, j: (0, 0)),
                  pl.BlockSpec((2 * B_HD, 1), lambda b, j: (0, 0))],
        out_specs=pl.BlockSpec((B_HEADS * LANES, TQ), lambda b, j: (0, qblk(b, j))),
        out_shape=jax.ShapeDtypeStruct((B_HEADS * LANES, t_all), BF16),
        scratch_shapes=[pltpu.VMEM((1, 2 * TQ), F32), pltpu.VMEM((1, 2 * TQ), F32),
                        pltpu.VMEM((LANES, 2 * TQ), F32)],
        compiler_params=_cparams(2),
        name="diff_attention",
    )(qb, kb, kb, vbt, vbt, lam_vec, gsub_col)


def _outproj_kernel(h_ref, mod_ref, a_ref, b_ref, w_ref, g_ref, beta_ref, o_ref, *, a_transposed):
    half = w_ref.shape[0] // 2
    ya = _dot_tn(a_ref[...], w_ref[0:half, :]) if a_transposed else _dot(a_ref[...], w_ref[0:half, :])
    y = ya + _dot_tn(b_ref[...], w_ref[half:2 * half, :])
    z = DN_ALPHA * h_ref[...] + mod_ref[0, 2:3, :] * y
    o_ref[...] = _layer_norm(z, g_ref[...], beta_ref[...])


def _outproj(h, mod3, mix_a, mix_b, w, g, b, n_tiles, mod_index, a_transposed):
    half = w.shape[0] // 2
    a_spec = (pl.BlockSpec((half, TM), lambda i: (0, i)) if a_transposed
              else pl.BlockSpec((TM, half), lambda i: (i, 0)))
    return pl.pallas_call(
        functools.partial(_outproj_kernel, a_transposed=a_transposed),
        grid=(n_tiles,),
        in_specs=[pl.BlockSpec((TM, D_MODEL), lambda i: (i, 0)),
                  pl.BlockSpec((1, 3, D_MODEL), lambda i: (mod_index(i), 0, 0)),
                  a_spec,
                  pl.BlockSpec((half, TM), lambda i: (0, i)),
                  _resident((2 * half, D_MODEL), lambda i: (0, 0)),
                  pl.BlockSpec((1, D_MODEL), lambda i: (0, 0)),
                  pl.BlockSpec((1, D_MODEL), lambda i: (0, 0))],
        out_specs=pl.BlockSpec((TM, D_MODEL), lambda i: (i, 0)),
        out_shape=jax.ShapeDtypeStruct((n_tiles * TM, D_MODEL), F32),
        compiler_params=_cparams(1),
        name="outproj_postnorm",
    )(h, mod3, mix_a, mix_b, w, g, b)


def _odd_proj_kernel(h_ref, mod_ref, win_ref, u_ref, q_ref, k_ref, vt_ref):
    h = h_ref[...]
    hm = (h * (1.0 + mod_ref[0, 1:2, :]) + mod_ref[0, 0:1, :]).astype(BF16)
    p = _dot(hm, win_ref[...])
    u_ref[...] = p[:, 0:C_WIDTH]
    for s in range(D_WIDTH // LANES):
        sl = slice(LANES * s, LANES * (s + 1))
        q_ref[s] = (p[:, C_WIDTH:C_WIDTH + D_WIDTH][:, sl] * D_SCALE).astype(BF16)
        k_ref[s] = p[:, C_WIDTH + D_WIDTH:C_WIDTH + 2 * D_WIDTH][:, sl].astype(BF16)
        vt = p[:, C_WIDTH + 2 * D_WIDTH:C_WIDTH + 3 * D_WIDTH][:, sl].T.astype(BF16)
        for i in range(TM // VB):
            vt_ref[s, i] = vt[:, VB * i:VB * (i + 1)]


def _odd_proj(h, mod3, win, n_tiles, mod_index):
    t_all = n_tiles * TM
    slabs = D_WIDTH // LANES
    return pl.pallas_call(
        _odd_proj_kernel,
        grid=(n_tiles,),
        in_specs=[pl.BlockSpec((TM, D_MODEL), lambda i: (i, 0)),
                  pl.BlockSpec((1, 3, D_MODEL), lambda i: (mod_index(i), 0, 0)),
                  _resident((D_MODEL, C_WIDTH + 3 * D_WIDTH), lambda i: (0, 0))],
        out_specs=[pl.BlockSpec((TM, C_WIDTH), lambda i: (i, 0)),
                   pl.BlockSpec((slabs, TM, LANES), lambda i: (0, i, 0)),
                   pl.BlockSpec((slabs, TM, LANES), lambda i: (0, i, 0)),
                   pl.BlockSpec((slabs, TM // VB, LANES, VB), lambda i: (0, i, 0, 0))],
        out_shape=[jax.ShapeDtypeStruct((t_all, C_WIDTH), F32),
                   jax.ShapeDtypeStruct((slabs, t_all, LANES), BF16),
                   jax.ShapeDtypeStruct((slabs, t_all, LANES), BF16),
                   jax.ShapeDtypeStruct((slabs, t_all // VB, LANES, VB), BF16)],
        compiler_params=_cparams(1),
        name="odd_proj",
    )(h, mod3, win)


def _pool_kernel(prev_ref, u_ref, next_ref, w_ref, ps_ref, o_ref, ext_ref, *, tiles_per_seq):
    t = pl.program_id(0) % tiles_per_seq
    n = tiles_per_seq * TM
    ext_ref[0:POOL_HALO, :] = jnp.where(t > 0, prev_ref[...], 0.0)
    ext_ref[POOL_HALO:POOL_HALO + TM, :] = u_ref[...]
    ext_ref[POOL_HALO + TM:2 * POOL_HALO + TM, :] = jnp.where(t < tiles_per_seq - 1, next_ref[...], 0.0)
    pos = t * TM + lax.broadcasted_iota(jnp.int32, (TM, 1), 0)
    for g, w in enumerate(C_WINDOWS):
        left = w // 2
        right = w - 1 - left
        lanes = slice(C_GW * g, C_GW * (g + 1))
        tot = ext_ref[POOL_HALO - left:POOL_HALO - left + TM, lanes]
        for k in range(1 - left, right + 1):
            tot = tot + ext_ref[POOL_HALO + k:POOL_HALO + k + TM, lanes]
        cnt = jnp.minimum(pos + right + 1, n) - jnp.maximum(pos - left, 0)
        pooled = tot / cnt.astype(F32) - u_ref[:, lanes]
        y = _dot(pooled.astype(BF16), w_ref[g])
        o_ref[:, lanes] = (y * ps_ref[:, lanes]).astype(BF16)


def _pool(u, w_pool, pool_scale, n_tiles, tiles_per_seq):
    per = TM // POOL_HALO
    last = u.shape[0] // POOL_HALO - 1
    return pl.pallas_call(
        functools.partial(_pool_kernel, tiles_per_seq=tiles_per_seq),
        grid=(n_tiles,),
        in_specs=[pl.BlockSpec((POOL_HALO, C_WIDTH), lambda i: (jnp.maximum(i * per - 1, 0), 0)),
                  pl.BlockSpec((TM, C_WIDTH), lambda i: (i, 0)),
                  pl.BlockSpec((POOL_HALO, C_WIDTH), lambda i: (jnp.minimum((i + 1) * per, last), 0)),
                  pl.BlockSpec((C_GROUPS, C_GW, C_GW), lambda i: (0, 0, 0)),
                  pl.BlockSpec((1, C_WIDTH), lambda i: (0, 0))],
        out_specs=pl.BlockSpec((TM, C_WIDTH), lambda i: (i, 0)),
        out_shape=jax.ShapeDtypeStruct((n_tiles * TM, C_WIDTH), BF16),
        scratch_shapes=[pltpu.VMEM((TM + 2 * POOL_HALO, C_WIDTH), F32)],
        compiler_params=_cparams(1),
        name="multiscale_pool",
    )(u, u, u, w_pool, pool_scale)


def _rpb_kernel(rpb_ref, o_ref):
    hd = pl.program_id(0)
    n_r, n_c = 2 * NA_ROWS - 1, 2 * NA_COLS - 1
    idx = (lax.broadcasted_iota(jnp.int32, (GRID_W, GRID_W), 0)
           - lax.broadcasted_iota(jnp.int32, (GRID_W, GRID_W), 1) + NA_COLS - 1)
    for dr in range(n_r):
        acc = jnp.zeros((GRID_W, GRID_W), F32)
        for j in range(n_c):
            acc = jnp.where(idx == j, rpb_ref[hd * n_r * n_c + dr * n_c + j], acc)
        o_ref[0, dr] = acc


def _rpb_blocks(rpb):
    n_r, n_c = 2 * NA_ROWS - 1, 2 * NA_COLS - 1
    return pl.pallas_call(
        _rpb_kernel,
        grid=(D_HEADS,),
        in_specs=[pl.BlockSpec(memory_space=pltpu.SMEM)],
        out_specs=pl.BlockSpec((1, n_r, GRID_W, GRID_W), lambda i: (i, 0, 0, 0)),
        out_shape=jax.ShapeDtypeStruct((D_HEADS, n_r, GRID_W, GRID_W), F32),
        compiler_params=_cparams(1),
        name="rpb_blocks",
    )(rpb.reshape(D_HEADS * n_r * n_c))


def _na_bias_layout(blocks, rows):
    qr_per = TQ // GRID_W
    g_of = (0, 1, rows // qr_per - 1)
    gb_of = tuple(min(max(g - 1, 0), rows // qr_per - 3) for g in g_of)
    ridx = np.zeros((NA_PATTERNS, NA_KEY_ROWS, qr_per), np.int32)
    rvalid = np.zeros((NA_PATTERNS, NA_KEY_ROWS, qr_per), bool)
    for p in range(NA_PATTERNS):
        for kr in range(NA_KEY_ROWS):
            for qr in range(qr_per):
                r = qr_per * g_of[p] + qr
                ka = qr_per * gb_of[p] + kr
                r0 = min(max(r - NA_ROWS // 2, 0), rows - NA_ROWS)
                rvalid[p, kr, qr] = r0 <= ka < r0 + NA_ROWS
                ridx[p, kr, qr] = min(max(ka - r + NA_ROWS - 1, 0), 2 * NA_ROWS - 2)
    j = np.arange(GRID_W)
    cs = np.clip(j - NA_COLS // 2, 0, GRID_W - NA_COLS)
    cvalid = (j[:, None] >= cs[None, :]) & (j[:, None] < cs[None, :] + NA_COLS)
    valid = rvalid[:, :, None, :, None] & cvalid[None, None, :, None, :]
    t = blocks[:, ridx]
    t = jnp.transpose(t, (1, 0, 2, 4, 3, 5))
    t = jnp.where(valid[:, None], t, NEG_INF)
    t = t.reshape(NA_PATTERNS, D_HEADS // 2, 2, NA_KEY_ROWS * GRID_W, TQ)
    t = jnp.transpose(t, (0, 1, 3, 2, 4))
    return t.reshape(NA_PATTERNS, D_HEADS // 2, NA_KEY_ROWS * GRID_W, 2 * TQ)


def _na_kernel(q_ref, kl_ref, kc_ref, vl_ref, vc_ref, bias_ref, o_ref, *, nqb):
    g = pl.program_id(2)
    gb = jnp.clip(g - 1, 0, nqb - 3)
    n_nb = NA_KEY_ROWS * GRID_W
    qbd = _block_diag_queries(q_ref[0])
    k_nb = kl_ref[0, pl.ds(pl.multiple_of(gb * TQ, TQ), n_nb), :]
    s_nb = _dot_nt(k_nb, qbd) + bias_ref[0, 0]
    s_cx = _dot_nt(kc_ref[0], qbd)
    mx = jnp.maximum(jnp.max(s_nb, axis=0, keepdims=True), jnp.max(s_cx, axis=0, keepdims=True))
    p_nb = jnp.exp(s_nb - mx)
    p_cx = jnp.exp(s_cx - mx)
    l = jnp.sum(p_nb, axis=0, keepdims=True) + jnp.sum(p_cx, axis=0, keepdims=True)
    pb = p_nb.astype(BF16)
    acc = _dot(vc_ref[0, 0], p_cx.astype(BF16))
    for i in range(n_nb // VB):
        acc = acc + _dot(vl_ref[0, gb + i], pb[VB * i:VB * (i + 1)])
    on = acc * (1.0 / l)
    half = LANES // 2
    o_ref[0:half, :] = on[0:half, 0:TQ].astype(BF16)
    o_ref[half:LANES, :] = on[half:LANES, TQ:2 * TQ].astype(BF16)


def _natten(qn, kn, vnt, bias, bsz, n, m):
    t_lat = bsz * n
    nqb = n // TQ
    slabs = D_WIDTH // LANES
    pat = lambda g: jnp.where(g == 0, 0, jnp.where(g == nqb - 1, 2, 1))
    return pl.pallas_call(
        functools.partial(_na_kernel, nqb=nqb),
        grid=(bsz, slabs, nqb),
        in_specs=[pl.BlockSpec((1, TQ, LANES), lambda b, s, g: (s, b * nqb + g, 0)),
                  pl.BlockSpec((1, n, LANES), lambda b, s, g: (s, b, 0)),
                  pl.BlockSpec((1, m, LANES), lambda b, s, g: (s, t_lat // m + b, 0)),
                  pl.BlockSpec((1, n // VB, LANES, VB), lambda b, s, g: (s, b, 0, 0)),
                  pl.BlockSpec((1, m // VB, LANES, VB), lambda b, s, g: (s, t_lat // m + b, 0, 0)),
                  pl.BlockSpec((1, 1, NA_KEY_ROWS * GRID_W, 2 * TQ), lambda b, s, g: (pat(g), s, 0, 0))],
        out_specs=pl.BlockSpec((LANES, TQ), lambda b, s, g: (s, b * nqb + g)),
        out_shape=jax.ShapeDtypeStruct((D_WIDTH, t_lat), BF16),
        compiler_params=_cparams(3),
        name="neighbourhood_attention",
    )(qn, kn, kn, vnt, vnt, bias)


def _rope_tables(n, rot_dim):
    t = np.arange(n)
    row = (t // GRID_W).astype(np.float32)
    col = (t % GRID_W).astype(np.float32)
    quarter = rot_dim // 4
    inv = jnp.asarray(ROPE_BASE, F32) ** (-jnp.arange(quarter, dtype=F32) / quarter)
    ang_r = jnp.asarray(row)[:, None] * inv[None, :]
    ang_c = jnp.asarray(col)[:, None] * inv[None, :]
    ang = jnp.concatenate([ang_r, ang_r, ang_c, ang_c], -1)
    sign = np.where((np.arange(rot_dim) % (2 * quarter)) < quarter, -1.0, 1.0).astype(np.float32)
    cos = jnp.tile(jnp.cos(ang), (1, LANES // rot_dim))
    sin = jnp.tile(jnp.sin(ang) * sign[None, :], (1, LANES // rot_dim))
    cos = jnp.concatenate([cos, jnp.ones((TM, LANES), F32)], 0)
    sin = jnp.concatenate([sin, jnp.zeros((TM, LANES), F32)], 0)
    return cos, sin


def _even_weights(w_in, w_uq, w_ukv):
    o = A_Q_RANK + B_QK
    kpe = jnp.pad(w_in[:, o + A_KV_RANK:o + A_KV_RANK + A_ROPE], ((0, 0), (0, LANES - A_ROPE)))
    win = jnp.concatenate([w_in[:, 0:A_Q_RANK], w_in[:, A_Q_RANK:o], w_in[:, o:o + A_KV_RANK], kpe,
                           w_in[:, o + A_KV_RANK + A_ROPE:o + A_KV_RANK + A_ROPE + B_QK],
                           w_in[:, o + A_KV_RANK + A_ROPE + B_QK:]], axis=1).astype(BF16)
    uq = w_uq.reshape(A_Q_RANK, A_HEADS, A_NOPE + A_ROPE)
    wuq = jnp.concatenate([uq[:, :, :A_NOPE].reshape(A_Q_RANK, A_HEADS * A_NOPE),
                           uq[:, :, A_NOPE:].reshape(A_Q_RANK, A_HEADS * A_ROPE)], axis=1).astype(BF16)
    ukv = w_ukv.reshape(A_KV_RANK, A_HEADS, A_NOPE + A_V)
    uk_t = jnp.transpose(ukv[:, :, :A_NOPE], (1, 2, 0))
    wabs = jnp.zeros((A_HEADS, A_NOPE, A_HEADS, A_KV_RANK), F32)
    wabs = wabs.at[np.arange(A_HEADS), :, np.arange(A_HEADS), :].set(uk_t)
    wabs = wabs.reshape(A_HEADS * A_NOPE, A_HEADS * A_KV_RANK).astype(BF16)
    wuvt = jnp.transpose(ukv[:, :, A_NOPE:], (1, 2, 0)).astype(BF16)
    return win, wuq, wabs, wuvt


def kernel(x, c, ctx, c_ctx, ada_w, ada_b, ln_g, ln_b, ffn_w_gate, ffn_w_up, ffn_w_down, ev_w_in, ev_w_out,
           ev_g_qlat, ev_g_kvlat, ev_w_uq, ev_w_ukv, ev_lam, ev_g_sub, od_w_in, od_w_out, od_w_pool,
           od_pool_scale, od_rpb):
    bsz, n, d = x.shape
    m = ctx.shape[1]
    assert d == D_MODEL and ada_w.shape[0] == DEPTH == 2
    assert m == TQ and n % TK == 0 and n % GRID_W == 0 and (bsz * m) % TM == 0 and bsz + 1 <= 8
    assert n // TQ >= 3
    t_lat = bsz * n
    t_all = t_lat + bsz * m
    tiles_lat, tiles_all, tiles_seq = t_lat // TM, t_all // TM, n // TM
    mod_index = lambda i: jnp.minimum(i // tiles_seq, bsz)
    tab_index = lambda i: jnp.where(i < tiles_lat, i % tiles_seq, tiles_seq)

    c_all = jnp.zeros((8, D_MODEL), F32).at[:bsz].set(c).at[bsz].set(c_ctx)
    mod = _modulation(c_all, ada_w, ada_b).reshape(DEPTH, 8, N_MOD, D_MODEL)
    h = jnp.concatenate([x.reshape(t_lat, D_MODEL), ctx.reshape(bsz * m, D_MODEL)], axis=0)
    row = lambda v: v.reshape(1, -1)
    wg, wu, wd = ffn_w_gate.astype(BF16), ffn_w_up.astype(BF16), ffn_w_down.astype(BF16)

    h = _ffn(h, mod[0, :, 0:3], wg[0, 0], wu[0, 0], wd[0, 0], row(ln_g[0, 0]), row(ln_b[0, 0]), tiles_all, mod_index)
    win, wuq, wabs, wuvt = _even_weights(ev_w_in[0], ev_w_uq[0], ev_w_ukv[0])
    tabs = _rope_tables(n, A_ROPE) + _rope_tables(n, B_HD)
    qa, ka, ct, qb, kb, vbt = _even_proj(h, mod[0, :, 3:6], win, row(ev_g_qlat[0]), row(ev_g_kvlat[0]), wuq, wabs,
                                         tabs, tiles_all, mod_index, tab_index)
    lam_init = 0.8 - 0.6 * math.exp(-0.3 * 0)
    mix_a = _mla(qa, ka, ct, wuvt, bsz, n, m)
    mix_b = _diff(qb, kb, vbt, ev_lam[0], ev_g_sub[0].reshape(2 * B_HD, 1), lam_init, bsz, n, m)
    h = _outproj(h, mod[0, :, 3:6], mix_a, mix_b, ev_w_out[0].astype(BF16), row(ln_g[0, 1]), row(ln_b[0, 1]),
                 tiles_all, mod_index, True)
    h = _ffn(h, mod[0, :, 6:9], wg[0, 1], wu[0, 1], wd[0, 1], row(ln_g[0, 2]), row(ln_b[0, 2]), tiles_all, mod_index)

    h = _ffn(h, mod[1, :, 0:3], wg[1, 0], wu[1, 0], wd[1, 0], row(ln_g[1, 0]), row(ln_b[1, 0]), tiles_all, mod_index)
    u, qn, kn, vnt = _odd_proj(h, mod[1, :, 3:6], od_w_in[0].astype(BF16), tiles_all, mod_index)
    y_pool = _pool(u, od_w_pool[0].astype(BF16), row(od_pool_scale[0]), tiles_lat, tiles_seq)
    bias = _na_bias_layout(_rpb_blocks(od_rpb[0]), n // GRID_W)
    y_na = _natten(qn, kn, vnt, bias, bsz, n, m)
    h = _outproj(h, mod[1, :, 3:6], y_pool, y_na, od_w_out[0].astype(BF16), row(ln_g[1, 1]), row(ln_b[1, 1]),
                 tiles_lat, mod_index, False)
    h = _ffn(h, mod[1, :, 6:9], wg[1, 1], wu[1, 1], wd[1, 1], row(ln_g[1, 2]), row(ln_b[1, 2]), tiles_lat, mod_index)
    return h.reshape(bsz, n, D_MODEL)
```

```python
import functools
import math

import numpy as np
import jax
import jax.numpy as jnp
from jax import lax
from jax.experimental import pallas as pl
from jax.experimental.pallas import tpu as pltpu

F32 = jnp.float32
BF16 = jnp.bfloat16

D_MODEL = 1024
DEPTH = 2
GRID_W = 64
N_MOD = 9
DN_ALPHA = float((2 * DEPTH) ** 0.25)
LN_EPS = 1e-6
RMS_EPS = 1e-6
SUBLN_EPS = 1e-5
ROPE_BASE = 10000.0
NEG_INF = -1e30
D_FF = 2816

A_HEADS = 8
A_NOPE = 64
A_ROPE = 32
A_V = 64
A_Q_RANK = 256
A_KV_RANK = 128
A_SCALE = float((A_NOPE + A_ROPE) ** -0.5)
A_QK = A_KV_RANK + A_ROPE

B_HEADS = 4
B_HD = 64
B_SCALE = float(B_HD ** -0.5)
B_QK = B_HEADS * 2 * B_HD

C_GROUPS = 4
C_WINDOWS = (2, 4, 8, 16)
C_GW = 128
C_WIDTH = C_GROUPS * C_GW

D_HEADS = 8
D_HD = 64
D_SCALE = float(D_HD ** -0.5)
D_WIDTH = D_HEADS * D_HD
NA_ROWS = 8
NA_COLS = 16

LOG2E = math.log2(math.e)
A_QSCALE = A_SCALE * LOG2E
B_QSCALE = B_SCALE * LOG2E

LANES = 128
SUBLANES = 8
TM = 512
TQ = 256
NQ = 2 * TQ
TK = 512
VB = 256
STRIP = 64
FF_CHUNK = 1408
POOL_HALO = 8
NA_KEY_ROWS = 12
NA_PATTERNS = 3
VMEM_LIMIT = 56 * 1024 * 1024


def _cparams(n_axes):
    return pltpu.CompilerParams(dimension_semantics=("arbitrary",) * n_axes,
                                vmem_limit_bytes=VMEM_LIMIT)


def _resident(shape, index_map):
    return pl.BlockSpec(shape, index_map, pipeline_mode=pl.Buffered(1))


def _layer_norm(z, g, b):
    mu = jnp.mean(z, axis=-1, keepdims=True)
    zc = z - mu
    var = jnp.mean(zc * zc, axis=-1, keepdims=True)
    return zc * lax.rsqrt(var + LN_EPS) * g + b


def _rms_norm(x, g, eps):
    return x * lax.rsqrt(jnp.mean(x * x, axis=-1, keepdims=True) + eps) * g


def _dot(a, b):
    return jnp.dot(a, b, preferred_element_type=F32)


def _dot_nt(a, b):
    return lax.dot_general(a, b, (((1,), (1,)), ((), ())), preferred_element_type=F32)


def _dot_tn(a, b):
    return lax.dot_general(a, b, (((0,), (0,)), ((), ())), preferred_element_type=F32)


def _mod_kernel(c_ref, w_ref, b_ref, o_ref):
    x = c_ref[...]
    s = (x * jax.nn.sigmoid(x)).astype(BF16)
    o_ref[0] = _dot(s, w_ref[0].astype(BF16)) + b_ref[0]


def _modulation(c_all, ada_w, ada_b):
    depth = ada_w.shape[0]
    return pl.pallas_call(
        _mod_kernel,
        grid=(depth, N_MOD),
        in_specs=[pl.BlockSpec((8, D_MODEL), lambda l, j: (0, 0)),
                  pl.BlockSpec((1, D_MODEL, D_MODEL), lambda l, j: (l, 0, j)),
                  pl.BlockSpec((1, 1, D_MODEL), lambda l, j: (l, 0, j))],
        out_specs=pl.BlockSpec((1, 8, D_MODEL), lambda l, j: (l, 0, j)),
        out_shape=jax.ShapeDtypeStruct((depth, 8, N_MOD * D_MODEL), F32),
        compiler_params=_cparams(2),
        name="modulation",
    )(c_all, ada_w, ada_b.reshape(depth, 1, N_MOD * D_MODEL))


def _ffn_kernel(h_ref, mod_ref, wg_ref, wu_ref, wd_ref, g_ref, b_ref, o_ref):
    h = h_ref[...]
    shift, scale, gate = mod_ref[0, 0:1, :], mod_ref[0, 1:2, :], mod_ref[0, 2:3, :]
    hm = (h * (1.0 + scale) + shift).astype(BF16)
    y = jnp.zeros((TM, D_MODEL), F32)
    for c in range(D_FF // FF_CHUNK):
        cols = slice(c * FF_CHUNK, (c + 1) * FF_CHUNK)
        gt = _dot(hm, wg_ref[:, cols])
        up = _dot(hm, wu_ref[:, cols])
        act = (gt * jax.nn.sigmoid(gt) * up).astype(BF16)
        y = y + _dot(act, wd_ref[cols, :])
    o_ref[...] = _layer_norm(DN_ALPHA * h + (0.5 * gate) * y, g_ref[...], b_ref[...])


def _ffn(h, mod3, wg, wu, wd, g, b, n_tiles, mod_index):
    return pl.pallas_call(
        _ffn_kernel,
        grid=(n_tiles,),
        in_specs=[pl.BlockSpec((TM, D_MODEL), lambda i: (i, 0)),
                  pl.BlockSpec((1, 3, D_MODEL), lambda i: (mod_index(i), 0, 0)),
                  _resident((D_MODEL, D_FF), lambda i: (0, 0)),
                  _resident((D_MODEL, D_FF), lambda i: (0, 0)),
                  _resident((D_FF, D_MODEL), lambda i: (0, 0)),
                  pl.BlockSpec((1, D_MODEL), lambda i: (0, 0)),
                  pl.BlockSpec((1, D_MODEL), lambda i: (0, 0))],
        out_specs=pl.BlockSpec((TM, D_MODEL), lambda i: (i, 0)),
        out_shape=jax.ShapeDtypeStruct((n_tiles * TM, D_MODEL), F32),
        compiler_params=_cparams(1),
        name="ffn_postnorm",
    )(h, mod3, wg, wu, wd, g, b)


def _rope(x, cos, sin_signed, quarter):
    lane = lax.broadcasted_iota(jnp.int32, x.shape, 1)
    first = (lane % (2 * quarter)) < quarter
    rot = jnp.where(first, pltpu.roll(x, LANES - quarter, 1), pltpu.roll(x, quarter, 1))
    return x * cos + rot * sin_signed


EV_QLAT = 0
EV_BQ = 256
EV_KVLAT = 768
EV_KPE = 896
EV_BK = 1024
EV_BV = 1536
EV_PCOLS = 2048


def _even_proj_kernel(h_ref, mod_ref, win_ref, gq_ref, gkv_ref, wuq_ref, wabs_ref,
                      cosa_ref, sina_ref, cosb_ref, sinb_ref,
                      qa_ref, ka_ref, ct_ref, qb_ref, kb_ref, vbt_ref):
    h = h_ref[...]
    hm = (h * (1.0 + mod_ref[0, 1:2, :]) + mod_ref[0, 0:1, :]).astype(BF16)
    p = _dot(hm, win_ref[...])
    cosa, sina, cosb, sinb = cosa_ref[...], sina_ref[...], cosb_ref[...], sinb_ref[...]

    qn = _rms_norm(p[:, EV_QLAT:EV_QLAT + A_Q_RANK], gq_ref[...], RMS_EPS).astype(BF16)
    q = _dot(qn, wuq_ref[...])
    n_nope = A_HEADS * A_NOPE
    qabs = _dot(q[:, :n_nope].astype(BF16), wabs_ref[...])
    qpe = [_rope(q[:, n_nope + LANES * s:n_nope + LANES * (s + 1)], cosa, sina, A_ROPE // 4) for s in range(2)]
    qpe_t = jnp.concatenate(qpe, axis=1).T
    for hd in range(A_HEADS):
        qa_ref[hd, 0:A_KV_RANK, :] = (qabs[:, A_KV_RANK * hd:A_KV_RANK * (hd + 1)].T * A_QSCALE).astype(BF16)
        qa_ref[hd, A_KV_RANK:A_QK, :] = (qpe_t[A_ROPE * hd:A_ROPE * (hd + 1), :] * A_QSCALE).astype(BF16)

    cn = _rms_norm(p[:, EV_KVLAT:EV_KVLAT + A_KV_RANK], gkv_ref[...], RMS_EPS)
    kpe = _rope(p[:, EV_KPE:EV_KPE + LANES], cosa, sina, A_ROPE // 4)
    ka_ref[:, 0:A_KV_RANK] = cn.astype(BF16)
    ka_ref[:, A_KV_RANK:A_QK] = kpe[:, 0:A_ROPE].astype(BF16)
    cnt = cn.T.astype(BF16)
    for s in range(TM // VB):
        ct_ref[s] = cnt[:, VB * s:VB * (s + 1)]

    for hd in range(B_HEADS):
        sl = slice(LANES * hd, LANES * (hd + 1))
        qb_ref[hd] = (_rope(p[:, EV_BQ:EV_BQ + B_QK][:, sl], cosb, sinb, B_HD // 4).T * B_QSCALE).astype(BF16)
        kb_ref[hd] = _rope(p[:, EV_BK:EV_BK + B_QK][:, sl], cosb, sinb, B_HD // 4).astype(BF16)
        vt = p[:, EV_BV:EV_BV + B_QK][:, sl].T.astype(BF16)
        for s in range(TM // VB):
            vbt_ref[hd, s] = vt[:, VB * s:VB * (s + 1)]


def _even_proj(h, mod3, win, gq, gkv, wuq, wabs, tabs, n_tiles, mod_index, tab_index):
    t_all = n_tiles * TM
    row = lambda i: (i, 0)
    tab = lambda i: (tab_index(i), 0)
    const = lambda i: (0, 0)
    return pl.pallas_call(
        _even_proj_kernel,
        grid=(n_tiles,),
        in_specs=[pl.BlockSpec((TM, D_MODEL), row),
                  pl.BlockSpec((1, 3, D_MODEL), lambda i: (mod_index(i), 0, 0)),
                  _resident((D_MODEL, EV_PCOLS), const),
                  pl.BlockSpec((1, A_Q_RANK), const),
                  pl.BlockSpec((1, A_KV_RANK), const),
                  _resident((A_Q_RANK, A_HEADS * (A_NOPE + A_ROPE)), const),
                  _resident((A_HEADS * A_NOPE, A_HEADS * A_KV_RANK), const),
                  pl.BlockSpec((TM, LANES), tab), pl.BlockSpec((TM, LANES), tab),
                  pl.BlockSpec((TM, LANES), tab), pl.BlockSpec((TM, LANES), tab)],
        out_specs=[pl.BlockSpec((A_HEADS, A_QK, TM), lambda i: (0, 0, i)),
                   pl.BlockSpec((TM, A_QK), row),
                   pl.BlockSpec((TM // VB, A_KV_RANK, VB), lambda i: (i, 0, 0)),
                   pl.BlockSpec((B_HEADS, LANES, TM), lambda i: (0, 0, i)),
                   pl.BlockSpec((B_HEADS, TM, LANES), lambda i: (0, i, 0)),
                   pl.BlockSpec((B_HEADS, TM // VB, LANES, VB), lambda i: (0, i, 0, 0))],
        out_shape=[jax.ShapeDtypeStruct((A_HEADS, A_QK, t_all), BF16),
                   jax.ShapeDtypeStruct((t_all, A_QK), BF16),
                   jax.ShapeDtypeStruct((t_all // VB, A_KV_RANK, VB), BF16),
                   jax.ShapeDtypeStruct((B_HEADS, LANES, t_all), BF16),
                   jax.ShapeDtypeStruct((B_HEADS, t_all, LANES), BF16),
                   jax.ShapeDtypeStruct((B_HEADS, t_all // VB, LANES, VB), BF16)],
        compiler_params=_cparams(1),
        name="even_proj",
    )(h, mod3, win, gq, gkv, wuq, wabs, *tabs)


class _Flash:
    def __init__(self, s_refs, p_refs, a_refs, m_ref, l_ref, acc_ref):
        self.s, self.p, self.a = s_refs, p_refs, a_refs
        self.m, self.l, self.acc = m_ref, l_ref, acc_ref

    def reset(self):
        self.m[...] = jnp.full(self.m.shape, NEG_INF, F32)
        self.l[...] = jnp.zeros(self.l.shape, F32)
        self.acc[...] = jnp.zeros(self.acc.shape, F32)

    def qk(self, slot, k, q_t):
        self.s[slot][0:k.shape[0], :] = _dot(k, q_t)

    def sm(self, slot, rows):
        s_ref, p_ref = self.s[slot], self.p[slot]
        groups = STRIP // SUBLANES
        m8 = None
        for r in range(rows // STRIP):
            part = jnp.max(s_ref[STRIP * r:STRIP * (r + 1), :].reshape(groups, SUBLANES, NQ), axis=0)
            m8 = part if m8 is None else jnp.maximum(m8, part)
        m_old = self.m[...]
        m_new = jnp.maximum(m_old, jnp.max(m8, axis=0, keepdims=True))
        alpha = jnp.exp2(m_old - m_new)
        self.m[...] = m_new
        self.a[slot][...] = alpha
        l8 = None
        for r in range(rows // STRIP):
            p = jnp.exp2(s_ref[STRIP * r:STRIP * (r + 1), :] - m_new)
            part = jnp.sum(p.reshape(groups, SUBLANES, NQ), axis=0)
            l8 = part if l8 is None else l8 + part
            p_ref[STRIP * r:STRIP * (r + 1), :] = p.astype(BF16)
        self.l[...] = alpha * self.l[...] + l8

    def pv(self, slot, vt_blocks):
        p_ref = self.p[slot]
        pv = _dot(vt_blocks[0], p_ref[0:VB, :])
        for i in range(1, len(vt_blocks)):
            pv = pv + _dot(vt_blocks[i], p_ref[VB * i:VB * (i + 1), :])
        self.acc[...] = self.a[slot][...] * self.acc[...] + pv

    def run_latent(self, q_t, k_lat, vt_lat, k_ctx, vt_ctx, n_lat):
        rows_c = k_ctx.shape[0]
        self.reset()
        self.qk(0, k_lat(0), q_t)
        self.qk(1, k_lat(1), q_t)
        self.sm(0, TK)

        def body(i, c):
            for u in range(2):
                t = 2 + 2 * i + u
                self.qk(u, k_lat(t), q_t)
                self.sm(1 - u, TK)
                self.pv(u, vt_lat(t - 2))
            return c

        lax.fori_loop(0, (n_lat - 2) // 2, body, 0)
        self.qk(0, k_ctx, q_t)
        self.sm(1, TK)
        self.pv(0, vt_lat(n_lat - 2))
        self.sm(0, rows_c)
        self.pv(1, vt_lat(n_lat - 1))
        self.pv(0, vt_ctx)

    def run_context(self, q_t, k_ctx, vt_ctx):
        self.reset()
        self.qk(0, k_ctx, q_t)
        self.sm(0, k_ctx.shape[0])
        self.pv(0, vt_ctx)

    def normalised(self):
        l = jnp.sum(self.l[...], axis=0, keepdims=True)
        return self.acc[...] * (1.0 / l)


def _flash_scratch(dv):
    return ([pltpu.VMEM((TK, NQ), F32)] * 2 + [pltpu.VMEM((TK, NQ), BF16)] * 2 + [pltpu.VMEM((1, NQ), F32)] * 2
            + [pltpu.VMEM((1, NQ), F32), pltpu.VMEM((SUBLANES, NQ), F32), pltpu.VMEM((dv, NQ), F32)])


def _block_diag_queries_t(q_t):
    row = lax.broadcasted_iota(jnp.int32, q_t.shape, 0)
    zero = jnp.zeros_like(q_t)
    half = LANES // 2
    return jnp.concatenate([jnp.where(row < half, q_t, zero), jnp.where(row >= half, q_t, zero)], axis=1)


def _block_diag_queries(q12):
    lane = lax.broadcasted_iota(jnp.int32, q12.shape, 1)
    zero = jnp.zeros_like(q12)
    half = LANES // 2
    return jnp.concatenate([jnp.where(lane < half, q12, zero), jnp.where(lane >= half, q12, zero)], axis=0)


def _mla_kernel(qa_ref, kl_ref, kc_ref, ctl_ref, ctc_ref, wuvt_ref, o_ref, s0, s1, p0, p1, a0, a1, m_ref, l_ref,
                acc_ref, *, n_lat_steps, n_qb_lat):
    fl = _Flash((s0, s1), (p0, p1), (a0, a1), m_ref, l_ref, acc_ref)
    k_lat = lambda t: kl_ref[pl.ds(pl.multiple_of(t * TK, TK), TK), :]
    vt_lat = lambda t: [ctl_ref[(TK // VB) * t + i] for i in range(TK // VB)]
    is_latent = pl.program_id(1) < n_qb_lat

    def pair(g, carry):
        q_t = jnp.concatenate([qa_ref[2 * g], qa_ref[2 * g + 1]], axis=1)

        @pl.when(is_latent)
        def _():
            fl.run_latent(q_t, k_lat, vt_lat, kc_ref[...], [ctc_ref[0]], n_lat_steps)

        @pl.when(jnp.logical_not(is_latent))
        def _():
            fl.run_context(q_t, kc_ref[...], [ctc_ref[0]])

        o_lat = fl.normalised().astype(BF16)
        for i in range(2):
            o = _dot(wuvt_ref[2 * g + i], o_lat[:, TQ * i:TQ * (i + 1)])
            o_ref[pl.ds(pl.multiple_of((2 * g + i) * A_V, A_V), A_V), :] = o.astype(BF16)
        return carry

    lax.fori_loop(0, A_HEADS // 2, pair, 0)


def _mla(qa, ka, ct, wuvt, bsz, n, m):
    t_lat = bsz * n
    t_all = t_lat + bsz * m
    nqb = n // TQ
    qblk = lambda b, j: jnp.where(j < nqb, b * nqb + j, t_lat // TQ + b)
    kern = functools.partial(_mla_kernel, n_lat_steps=n // TK, n_qb_lat=nqb)
    return pl.pallas_call(
        kern,
        grid=(bsz, nqb + 1),
        in_specs=[pl.BlockSpec((A_HEADS, A_QK, TQ), lambda b, j: (0, 0, qblk(b, j))),
                  _resident((n, A_QK), lambda b, j: (b, 0)),
                  pl.BlockSpec((m, A_QK), lambda b, j: (t_lat // m + b, 0)),
                  _resident((n // VB, A_KV_RANK, VB), lambda b, j: (b, 0, 0)),
                  pl.BlockSpec((m // VB, A_KV_RANK, VB), lambda b, j: (t_lat // m + b, 0, 0)),
                  pl.BlockSpec((A_HEADS, A_V, A_KV_RANK), lambda b, j: (0, 0, 0))],
        out_specs=pl.BlockSpec((A_HEADS * A_V, TQ), lambda b, j: (0, qblk(b, j))),
        out_shape=jax.ShapeDtypeStruct((A_HEADS * A_V, t_all), BF16),
        scratch_shapes=_flash_scratch(A_KV_RANK),
        compiler_params=_cparams(2),
        name="mla_attention",
    )(qa, ka, ka, ct, ct, wuvt)


def _diff_kernel(qb_ref, kl_ref, kc_ref, vl_ref, vc_ref, lam_ref, gsub_ref, o_ref, s0, s1, p0, p1, a0, a1, m_ref,
                 l_ref, acc_ref, *, n_lat_steps, n_qb_lat, lam_init):
    fl = _Flash((s0, s1), (p0, p1), (a0, a1), m_ref, l_ref, acc_ref)
    is_latent = pl.program_id(1) < n_qb_lat
    lv = lam_ref[...]
    lam = (jnp.exp(jnp.sum(lv[0:1] * lv[1:2], axis=1, keepdims=True))
           - jnp.exp(jnp.sum(lv[2:3] * lv[3:4], axis=1, keepdims=True)) + lam_init)

    def head(hd, carry):
        q_t = _block_diag_queries_t(qb_ref[hd])
        k_lat = lambda t: kl_ref[hd, pl.ds(pl.multiple_of(t * TK, TK), TK), :]
        vt_lat = lambda t: [vl_ref[hd, (TK // VB) * t + i] for i in range(TK // VB)]

        @pl.when(is_latent)
        def _():
            fl.run_latent(q_t, k_lat, vt_lat, kc_ref[hd], [vc_ref[hd, 0]], n_lat_steps)

        @pl.when(jnp.logical_not(is_latent))
        def _():
            fl.run_context(q_t, kc_ref[hd], [vc_ref[hd, 0]])

        on = fl.normalised()
        o = on[:, 0:TQ] - lam * on[:, TQ:2 * TQ]
        o = o * lax.rsqrt(jnp.mean(o * o, axis=0, keepdims=True) + SUBLN_EPS) * gsub_ref[...]
        o_ref[pl.ds(pl.multiple_of(hd * LANES, LANES), LANES), :] = (o * (1.0 - lam_init)).astype(BF16)
        return carry

    lax.fori_loop(0, B_HEADS, head, 0)


def _diff(qb, kb, vbt, lam_vec, gsub_col, lam_init, bsz, n, m):
    t_lat = bsz * n
    t_all = t_lat + bsz * m
    nqb = n // TQ
    qblk = lambda b, j: jnp.where(j < nqb, b * nqb + j, t_lat // TQ + b)
    kern = functools.partial(_diff_kernel, n_lat_steps=n // TK, n_qb_lat=nqb, lam_init=lam_init)
    return pl.pallas_call(
        kern,
        grid=(bsz, nqb + 1),
        in_specs=[pl.BlockSpec((B_HEADS, LANES, TQ), lambda b, j: (0, 0, qblk(b, j))),
                  _resident((B_HEADS, n, LANES), lambda b, j: (0, b, 0)),
                  pl.BlockSpec((B_HEADS, m, LANES), lambda b, j: (0, t_lat // m + b, 0)),
                  _resident((B_HEADS, n // VB, LANES, VB), lambda b, j: (0, b, 0, 0)),
                  pl.BlockSpec((B_HEADS, m // VB, LANES, VB), lambda b, j: (0, t_lat // m + b, 0, 0)),
                  pl.BlockSpec((4, B_HD), lambda b, j: (0, 0)),
                  pl.BlockSpec((2 * B_HD, 1), lambda b, j: (0, 0))],
        out_specs=pl.BlockSpec((B_HEADS * LANES, TQ), lambda b, j: (0, qblk(b, j))),
        out_shape=jax.ShapeDtypeStruct((B_HEADS * LANES, t_all), BF16),
        scratch_shapes=_flash_scratch(LANES),
        compiler_params=_cparams(2),
        name="diff_attention",
    )(qb, kb, kb, vbt, vbt, lam_vec, gsub_col)


def _outproj_kernel(h_ref, mod_ref, a_ref, b_ref, w_ref, g_ref, beta_ref, o_ref, *, a_transposed):
    half = w_ref.shape[0] // 2
    ya = _dot_tn(a_ref[...], w_ref[0:half, :]) if a_transposed else _dot(a_ref[...], w_ref[0:half, :])
    y = ya + _dot_tn(b_ref[...], w_ref[half:2 * half, :])
    z = DN_ALPHA * h_ref[...] + mod_ref[0, 2:3, :] * y
    o_ref[...] = _layer_norm(z, g_ref[...], beta_ref[...])


def _outproj(h, mod3, mix_a, mix_b, w, g, b, n_tiles, mod_index, a_transposed):
    half = w.shape[0] // 2
    a_spec = (pl.BlockSpec((half, TM), lambda i: (0, i)) if a_transposed
              else pl.BlockSpec((TM, half), lambda i: (i, 0)))
    return pl.pallas_call(
        functools.partial(_outproj_kernel, a_transposed=a_transposed),
        grid=(n_tiles,),
        in_specs=[pl.BlockSpec((TM, D_MODEL), lambda i: (i, 0)),
                  pl.BlockSpec((1, 3, D_MODEL), lambda i: (mod_index(i), 0, 0)),
                  a_spec,
                  pl.BlockSpec((half, TM), lambda i: (0, i)),
                  _resident((2 * half, D_MODEL), lambda i: (0, 0)),
                  pl.BlockSpec((1, D_MODEL), lambda i: (0, 0)),
                  pl.BlockSpec((1, D_MODEL), lambda i: (0, 0))],
        out_specs=pl.BlockSpec((TM, D_MODEL), lambda i: (i, 0)),
        out_shape=jax.ShapeDtypeStruct((n_tiles * TM, D_MODEL), F32),
        compiler_params=_cparams(1),
        name="outproj_postnorm",
    )(h, mod3, mix_a, mix_b, w, g, b)


def _odd_proj_kernel(h_ref, mod_ref, win_ref, u_ref, q_ref, k_ref, vt_ref):
    h = h_ref[...]
    hm = (h * (1.0 + mod_ref[0, 1:2, :]) + mod_ref[0, 0:1, :]).astype(BF16)
    p = _dot(hm, win_ref[...])
    u_ref[...] = p[:, 0:C_WIDTH]
    for s in range(D_WIDTH // LANES):
        sl = slice(LANES * s, LANES * (s + 1))
        q_ref[s] = (p[:, C_WIDTH:C_WIDTH + D_WIDTH][:, sl] * D_SCALE).astype(BF16)
        k_ref[s] = p[:, C_WIDTH + D_WIDTH:C_WIDTH + 2 * D_WIDTH][:, sl].astype(BF16)
        vt = p[:, C_WIDTH + 2 * D_WIDTH:C_WIDTH + 3 * D_WIDTH][:, sl].T.astype(BF16)
        for i in range(TM // VB):
            vt_ref[s, i] = vt[:, VB * i:VB * (i + 1)]


def _odd_proj(h, mod3, win, n_tiles, mod_index):
    t_all = n_tiles * TM
    slabs = D_WIDTH // LANES
    return pl.pallas_call(
        _odd_proj_kernel,
        grid=(n_tiles,),
        in_specs=[pl.BlockSpec((TM, D_MODEL), lambda i: (i, 0)),
                  pl.BlockSpec((1, 3, D_MODEL), lambda i: (mod_index(i), 0, 0)),
                  _resident((D_MODEL, C_WIDTH + 3 * D_WIDTH), lambda i: (0, 0))],
        out_specs=[pl.BlockSpec((TM, C_WIDTH), lambda i: (i, 0)),
                   pl.BlockSpec((slabs, TM, LANES), lambda i: (0, i, 0)),
                   pl.BlockSpec((slabs, TM, LANES), lambda i: (0, i, 0)),
                   pl.BlockSpec((slabs, TM // VB, LANES, VB), lambda i: (0, i, 0, 0))],
        out_shape=[jax.ShapeDtypeStruct((t_all, C_WIDTH), F32),
                   jax.ShapeDtypeStruct((slabs, t_all, LANES), BF16),
                   jax.ShapeDtypeStruct((slabs, t_all, LANES), BF16),
                   jax.ShapeDtypeStruct((slabs, t_all // VB, LANES, VB), BF16)],
        compiler_params=_cparams(1),
        name="odd_proj",
    )(h, mod3, win)


def _pool_kernel(prev_ref, u_ref, next_ref, w_ref, ps_ref, o_ref, ext_ref, *, tiles_per_seq):
    t = pl.program_id(0) % tiles_per_seq
    n = tiles_per_seq * TM
    ext_ref[0:POOL_HALO, :] = jnp.where(t > 0, prev_ref[...], 0.0)
    ext_ref[POOL_HALO:POOL_HALO + TM, :] = u_ref[...]
    ext_ref[POOL_HALO + TM:2 * POOL_HALO + TM, :] = jnp.where(t < tiles_per_seq - 1, next_ref[...], 0.0)
    pos = t * TM + lax.broadcasted_iota(jnp.int32, (TM, 1), 0)
    for g, w in enumerate(C_WINDOWS):
        left = w // 2
        right = w - 1 - left
        lanes = slice(C_GW * g, C_GW * (g + 1))
        tot = ext_ref[POOL_HALO - left:POOL_HALO - left + TM, lanes]
        for k in range(1 - left, right + 1):
            tot = tot + ext_ref[POOL_HALO + k:POOL_HALO + k + TM, lanes]
        cnt = jnp.minimum(pos + right + 1, n) - jnp.maximum(pos - left, 0)
        pooled = tot / cnt.astype(F32) - u_ref[:, lanes]
        y = _dot(pooled.astype(BF16), w_ref[g])
        o_ref[:, lanes] = (y * ps_ref[:, lanes]).astype(BF16)


def _pool(u, w_pool, pool_scale, n_tiles, tiles_per_seq):
    per = TM // POOL_HALO
    last = u.shape[0] // POOL_HALO - 1
    return pl.pallas_call(
        functools.partial(_pool_kernel, tiles_per_seq=tiles_per_seq),
        grid=(n_tiles,),
        in_specs=[pl.BlockSpec((POOL_HALO, C_WIDTH), lambda i: (jnp.maximum(i * per - 1, 0), 0)),
                  pl.BlockSpec((TM, C_WIDTH), lambda i: (i, 0)),
                  pl.BlockSpec((POOL_HALO, C_WIDTH), lambda i: (jnp.minimum((i + 1) * per, last), 0)),
                  pl.BlockSpec((C_GROUPS, C_GW, C_GW), lambda i: (0, 0, 0)),
                  pl.BlockSpec((1, C_WIDTH), lambda i: (0, 0))],
        out_specs=pl.BlockSpec((TM, C_WIDTH), lambda i: (i, 0)),
        out_shape=jax.ShapeDtypeStruct((n_tiles * TM, C_WIDTH), BF16),
        scratch_shapes=[pltpu.VMEM((TM + 2 * POOL_HALO, C_WIDTH), F32)],
        compiler_params=_cparams(1),
        name="multiscale_pool",
    )(u, u, u, w_pool, pool_scale)


def _rpb_kernel(rpb_ref, o_ref):
    hd = pl.program_id(0)
    n_r, n_c = 2 * NA_ROWS - 1, 2 * NA_COLS - 1
    idx = (lax.broadcasted_iota(jnp.int32, (GRID_W, GRID_W), 0)
           - lax.broadcasted_iota(jnp.int32, (GRID_W, GRID_W), 1) + NA_COLS - 1)
    for dr in range(n_r):
        acc = jnp.zeros((GRID_W, GRID_W), F32)
        for j in range(n_c):
            acc = jnp.where(idx == j, rpb_ref[hd * n_r * n_c + dr * n_c + j], acc)
        o_ref[0, dr] = acc


def _rpb_blocks(rpb):
    n_r, n_c = 2 * NA_ROWS - 1, 2 * NA_COLS - 1
    return pl.pallas_call(
        _rpb_kernel,
        grid=(D_HEADS,),
        in_specs=[pl.BlockSpec(memory_space=pltpu.SMEM)],
        out_specs=pl.BlockSpec((1, n_r, GRID_W, GRID_W), lambda i: (i, 0, 0, 0)),
        out_shape=jax.ShapeDtypeStruct((D_HEADS, n_r, GRID_W, GRID_W), F32),
        compiler_params=_cparams(1),
        name="rpb_blocks",
    )(rpb.reshape(D_HEADS * n_r * n_c))


def _na_bias_layout(blocks, rows):
    qr_per = TQ // GRID_W
    g_of = (0, 1, rows // qr_per - 1)
    gb_of = tuple(min(max(g - 1, 0), rows // qr_per - 3) for g in g_of)
    ridx = np.zeros((NA_PATTERNS, NA_KEY_ROWS, qr_per), np.int32)
    rvalid = np.zeros((NA_PATTERNS, NA_KEY_ROWS, qr_per), bool)
    for p in range(NA_PATTERNS):
        for kr in range(NA_KEY_ROWS):
            for qr in range(qr_per):
                r = qr_per * g_of[p] + qr
                ka = qr_per * gb_of[p] + kr
                r0 = min(max(r - NA_ROWS // 2, 0), rows - NA_ROWS)
                rvalid[p, kr, qr] = r0 <= ka < r0 + NA_ROWS
                ridx[p, kr, qr] = min(max(ka - r + NA_ROWS - 1, 0), 2 * NA_ROWS - 2)
    j = np.arange(GRID_W)
    cs = np.clip(j - NA_COLS // 2, 0, GRID_W - NA_COLS)
    cvalid = (j[:, None] >= cs[None, :]) & (j[:, None] < cs[None, :] + NA_COLS)
    valid = rvalid[:, :, None, :, None] & cvalid[None, None, :, None, :]
    t = blocks[:, ridx]
    t = jnp.transpose(t, (1, 0, 2, 4, 3, 5))
    t = jnp.where(valid[:, None], t, NEG_INF)
    t = t.reshape(NA_PATTERNS, D_HEADS // 2, 2, NA_KEY_ROWS * GRID_W, TQ)
    t = jnp.transpose(t, (0, 1, 3, 2, 4))
    return t.reshape(NA_PATTERNS, D_HEADS // 2, NA_KEY_ROWS * GRID_W, 2 * TQ)


def _na_kernel(q_ref, kl_ref, kc_ref, vl_ref, vc_ref, bias_ref, o_ref, *, nqb):
    g = pl.program_id(2)
    gb = jnp.clip(g - 1, 0, nqb - 3)
    n_nb = NA_KEY_ROWS * GRID_W
    qbd = _block_diag_queries(q_ref[0])
    k_nb = kl_ref[0, pl.ds(pl.multiple_of(gb * TQ, TQ), n_nb), :]
    s_nb = _dot_nt(k_nb, qbd) + bias_ref[0, 0]
    s_cx = _dot_nt(kc_ref[0], qbd)
    mx = jnp.maximum(jnp.max(s_nb, axis=0, keepdims=True), jnp.max(s_cx, axis=0, keepdims=True))
    p_nb = jnp.exp(s_nb - mx)
    p_cx = jnp.exp(s_cx - mx)
    l = jnp.sum(p_nb, axis=0, keepdims=True) + jnp.sum(p_cx, axis=0, keepdims=True)
    pb = p_nb.astype(BF16)
    acc = _dot(vc_ref[0, 0], p_cx.astype(BF16))
    for i in range(n_nb // VB):
        acc = acc + _dot(vl_ref[0, gb + i], pb[VB * i:VB * (i + 1)])
    on = acc * (1.0 / l)
    half = LANES // 2
    o_ref[0:half, :] = on[0:half, 0:TQ].astype(BF16)
    o_ref[half:LANES, :] = on[half:LANES, TQ:2 * TQ].astype(BF16)


def _natten(qn, kn, vnt, bias, bsz, n, m):
    t_lat = bsz * n
    nqb = n // TQ
    slabs = D_WIDTH // LANES
    pat = lambda g: jnp.where(g == 0, 0, jnp.where(g == nqb - 1, 2, 1))
    return pl.pallas_call(
        functools.partial(_na_kernel, nqb=nqb),
        grid=(bsz, slabs, nqb),
        in_specs=[pl.BlockSpec((1, TQ, LANES), lambda b, s, g: (s, b * nqb + g, 0)),
                  pl.BlockSpec((1, n, LANES), lambda b, s, g: (s, b, 0)),
                  pl.BlockSpec((1, m, LANES), lambda b, s, g: (s, t_lat // m + b, 0)),
                  pl.BlockSpec((1, n // VB, LANES, VB), lambda b, s, g: (s, b, 0, 0)),
                  pl.BlockSpec((1, m // VB, LANES, VB), lambda b, s, g: (s, t_lat // m + b, 0, 0)),
                  pl.BlockSpec((1, 1, NA_KEY_ROWS * GRID_W, 2 * TQ), lambda b, s, g: (pat(g), s, 0, 0))],
        out_specs=pl.BlockSpec((LANES, TQ), lambda b, s, g: (s, b * nqb + g)),
        out_shape=jax.ShapeDtypeStruct((D_WIDTH, t_lat), BF16),
        compiler_params=_cparams(3),
        name="neighbourhood_attention",
    )(qn, kn, kn, vnt, vnt, bias)


def _rope_tables(n, rot_dim):
    t = np.arange(n)
    row = (t // GRID_W).astype(np.float32)
    col = (t % GRID_W).astype(np.float32)
    quarter = rot_dim // 4
    inv = jnp.asarray(ROPE_BASE, F32) ** (-jnp.arange(quarter, dtype=F32) / quarter)
    ang_r = jnp.asarray(row)[:, None] * inv[None, :]
    ang_c = jnp.asarray(col)[:, None] * inv[None, :]
    ang = jnp.concatenate([ang_r, ang_r, ang_c, ang_c], -1)
    sign = np.where((np.arange(rot_dim) % (2 * quarter)) < quarter, -1.0, 1.0).astype(np.float32)
    cos = jnp.tile(jnp.cos(ang), (1, LANES // rot_dim))
    sin = jnp.tile(jnp.sin(ang) * sign[None, :], (1, LANES // rot_dim))
    cos = jnp.concatenate([cos, jnp.ones((TM, LANES), F32)], 0)
    sin = jnp.concatenate([sin, jnp.zeros((TM, LANES), F32)], 0)
    return cos, sin


def _even_weights(w_in, w_uq, w_ukv):
    o = A_Q_RANK + B_QK
    kpe = jnp.pad(w_in[:, o + A_KV_RANK:o + A_KV_RANK + A_ROPE], ((0, 0), (0, LANES - A_ROPE)))
    win = jnp.concatenate([w_in[:, 0:A_Q_RANK], w_in[:, A_Q_RANK:o], w_in[:, o:o + A_KV_RANK], kpe,
                           w_in[:, o + A_KV_RANK + A_ROPE:o + A_KV_RANK + A_ROPE + B_QK],
                           w_in[:, o + A_KV_RANK + A_ROPE + B_QK:]], axis=1).astype(BF16)
    uq = w_uq.reshape(A_Q_RANK, A_HEADS, A_NOPE + A_ROPE)
    wuq = jnp.concatenate([uq[:, :, :A_NOPE].reshape(A_Q_RANK, A_HEADS * A_NOPE),
                           uq[:, :, A_NOPE:].reshape(A_Q_RANK, A_HEADS * A_ROPE)], axis=1).astype(BF16)
    ukv = w_ukv.reshape(A_KV_RANK, A_HEADS, A_NOPE + A_V)
    uk_t = jnp.transpose(ukv[:, :, :A_NOPE], (1, 2, 0))
    wabs = jnp.zeros((A_HEADS, A_NOPE, A_HEADS, A_KV_RANK), F32)
    wabs = wabs.at[np.arange(A_HEADS), :, np.arange(A_HEADS), :].set(uk_t)
    wabs = wabs.reshape(A_HEADS * A_NOPE, A_HEADS * A_KV_RANK).astype(BF16)
    wuvt = jnp.transpose(ukv[:, :, A_NOPE:], (1, 2, 0)).astype(BF16)
    return win, wuq, wabs, wuvt


def kernel(x, c, ctx, c_ctx, ada_w, ada_b, ln_g, ln_b, ffn_w_gate, ffn_w_up, ffn_w_down, ev_w_in, ev_w_out,
           ev_g_qlat, ev_g_kvlat, ev_w_uq, ev_w_ukv, ev_lam, ev_g_sub, od_w_in, od_w_out, od_w_pool,
           od_pool_scale, od_rpb):
    bsz, n, d = x.shape
    m = ctx.shape[1]
    assert d == D_MODEL and ada_w.shape[0] == DEPTH == 2
    assert m == TQ and n % (2 * TK) == 0 and n % GRID_W == 0 and (bsz * m) % TM == 0 and bsz + 1 <= 8
    assert n // TQ >= 3
    t_lat = bsz * n
    t_all = t_lat + bsz * m
    tiles_lat, tiles_all, tiles_seq = t_lat // TM, t_all // TM, n // TM
    mod_index = lambda i: jnp.minimum(i // tiles_seq, bsz)
    tab_index = lambda i: jnp.where(i < tiles_lat, i % tiles_seq, tiles_seq)

    c_all = jnp.zeros((8, D_MODEL), F32).at[:bsz].set(c).at[bsz].set(c_ctx)
    mod = _modulation(c_all, ada_w, ada_b).reshape(DEPTH, 8, N_MOD, D_MODEL)
    h = jnp.concatenate([x.reshape(t_lat, D_MODEL), ctx.reshape(bsz * m, D_MODEL)], axis=0)
    row = lambda v: v.reshape(1, -1)
    wg, wu, wd = ffn_w_gate.astype(BF16), ffn_w_up.astype(BF16), ffn_w_down.astype(BF16)

    h = _ffn(h, mod[0, :, 0:3], wg[0, 0], wu[0, 0], wd[0, 0], row(ln_g[0, 0]), row(ln_b[0, 0]), tiles_all, mod_index)
    win, wuq, wabs, wuvt = _even_weights(ev_w_in[0], ev_w_uq[0], ev_w_ukv[0])
    tabs = _rope_tables(n, A_ROPE) + _rope_tables(n, B_HD)
    qa, ka, ct, qb, kb, vbt = _even_proj(h, mod[0, :, 3:6], win, row(ev_g_qlat[0]), row(ev_g_kvlat[0]), wuq, wabs,
                                         tabs, tiles_all, mod_index, tab_index)
    lam_init = 0.8 - 0.6 * math.exp(-0.3 * 0)
    mix_a = _mla(qa, ka, ct, wuvt, bsz, n, m)
    mix_b = _diff(qb, kb, vbt, ev_lam[0], ev_g_sub[0].reshape(2 * B_HD, 1), lam_init, bsz, n, m)
    h = _outproj(h, mod[0, :, 3:6], mix_a, mix_b, ev_w_out[0].astype(BF16), row(ln_g[0, 1]), row(ln_b[0, 1]),
                 tiles_all, mod_index, True)
    h = _ffn(h, mod[0, :, 6:9], wg[0, 1], wu[0, 1], wd[0, 1], row(ln_g[0, 2]), row(ln_b[0, 2]), tiles_all, mod_index)

    h = _ffn(h, mod[1, :, 0:3], wg[1, 0], wu[1, 0], wd[1, 0], row(ln_g[1, 0]), row(ln_b[1, 0]), tiles_all, mod_index)
    u, qn, kn, vnt = _odd_proj(h, mod[1, :, 3:6], od_w_in[0].astype(BF16), tiles_all, mod_index)
    y_pool = _pool(u, od_w_pool[0].astype(BF16), row(od_pool_scale[0]), tiles_lat, tiles_seq)
    bias = _na_bias_layout(_rpb_blocks(od_rpb[0]), n // GRID_W)
    y_na = _natten(qn, kn, vnt, bias, bsz, n, m)
    h = _outproj(h, mod[1, :, 3:6], y_pool, y_na, od_w_out[0].astype(BF16), row(ln_g[1, 1]), row(ln_b[1, 1]),
                 tiles_lat, mod_index, False)
    h = _ffn(h, mod[1, :, 6:9], wg[1, 1], wu[1, 1], wd[1, 1], row(ln_g[1, 2]), row(ln_b[1, 2]), tiles_lat, mod_index)
    return h.reshape(bsz, n, D_MODEL)
```

```python
import functools
import math

import numpy as np
import jax
import jax.numpy as jnp
from jax import lax
from jax.experimental import pallas as pl
from jax.experimental.pallas import tpu as pltpu

F32 = jnp.float32
BF16 = jnp.bfloat16

D_MODEL = 1024
DEPTH = 2
GRID_W = 64
N_MOD = 9
DN_ALPHA = float((2 * DEPTH) ** 0.25)
LN_EPS = 1e-6
RMS_EPS = 1e-6
SUBLN_EPS = 1e-5
ROPE_BASE = 10000.0
NEG_INF = -1e30
D_FF = 2816

A_HEADS = 8
A_NOPE = 64
A_ROPE = 32
A_V = 64
A_Q_RANK = 256
A_KV_RANK = 128
A_SCALE = float((A_NOPE + A_ROPE) ** -0.5)
A_QK = A_KV_RANK + A_ROPE

B_HEADS = 4
B_HD = 64
B_SCALE = float(B_HD ** -0.5)
B_QK = B_HEADS * 2 * B_HD

C_GROUPS = 4
C_WINDOWS = (2, 4, 8, 16)
C_GW = 128
C_WIDTH = C_GROUPS * C_GW

D_HEADS = 8
D_HD = 64
D_SCALE = float(D_HD ** -0.5)
D_WIDTH = D_HEADS * D_HD
NA_ROWS = 8
NA_COLS = 16

LOG2E = math.log2(math.e)
A_QSCALE = A_SCALE * LOG2E
B_QSCALE = B_SCALE * LOG2E

LANES = 128
SUBLANES = 8
TM = 512
TQ = 256
NQ = 2 * TQ
TK = 512
VB = 256
STRIP = 64
BF16_ROWS = 16
DV = LANES + BF16_ROWS
FF_CHUNK = 1408
POOL_HALO = 8
NA_KEY_ROWS = 12
NA_PATTERNS = 3
VMEM_LIMIT = 56 * 1024 * 1024


def _cparams(n_axes):
    return pltpu.CompilerParams(dimension_semantics=("arbitrary",) * n_axes,
                                vmem_limit_bytes=VMEM_LIMIT)


def _resident(shape, index_map):
    return pl.BlockSpec(shape, index_map, pipeline_mode=pl.Buffered(1))


def _layer_norm(z, g, b):
    mu = jnp.mean(z, axis=-1, keepdims=True)
    zc = z - mu
    var = jnp.mean(zc * zc, axis=-1, keepdims=True)
    return zc * lax.rsqrt(var + LN_EPS) * g + b


def _rms_norm(x, g, eps):
    return x * lax.rsqrt(jnp.mean(x * x, axis=-1, keepdims=True) + eps) * g


def _dot(a, b):
    return jnp.dot(a, b, preferred_element_type=F32)


def _dot_nt(a, b):
    return lax.dot_general(a, b, (((1,), (1,)), ((), ())), preferred_element_type=F32)


def _dot_tn(a, b):
    return lax.dot_general(a, b, (((0,), (0,)), ((), ())), preferred_element_type=F32)


def _mod_kernel(c_ref, w_ref, b_ref, o_ref):
    x = c_ref[...]
    s = (x * jax.nn.sigmoid(x)).astype(BF16)
    o_ref[0] = _dot(s, w_ref[0].astype(BF16)) + b_ref[0]


def _modulation(c_all, ada_w, ada_b):
    depth = ada_w.shape[0]
    return pl.pallas_call(
        _mod_kernel,
        grid=(depth, N_MOD),
        in_specs=[pl.BlockSpec((8, D_MODEL), lambda l, j: (0, 0)),
                  pl.BlockSpec((1, D_MODEL, D_MODEL), lambda l, j: (l, 0, j)),
                  pl.BlockSpec((1, 1, D_MODEL), lambda l, j: (l, 0, j))],
        out_specs=pl.BlockSpec((1, 8, D_MODEL), lambda l, j: (l, 0, j)),
        out_shape=jax.ShapeDtypeStruct((depth, 8, N_MOD * D_MODEL), F32),
        compiler_params=_cparams(2),
        name="modulation",
    )(c_all, ada_w, ada_b.reshape(depth, 1, N_MOD * D_MODEL))


def _ffn_kernel(h_ref, mod_ref, wg_ref, wu_ref, wd_ref, g_ref, b_ref, o_ref):
    h = h_ref[...]
    shift, scale, gate = mod_ref[0, 0:1, :], mod_ref[0, 1:2, :], mod_ref[0, 2:3, :]
    hm = (h * (1.0 + scale) + shift).astype(BF16)
    y = jnp.zeros((TM, D_MODEL), F32)
    for c in range(D_FF // FF_CHUNK):
        cols = slice(c * FF_CHUNK, (c + 1) * FF_CHUNK)
        gt = _dot(hm, wg_ref[:, cols])
        up = _dot(hm, wu_ref[:, cols])
        act = (gt * jax.nn.sigmoid(gt) * up).astype(BF16)
        y = y + _dot(act, wd_ref[cols, :])
    o_ref[...] = _layer_norm(DN_ALPHA * h + (0.5 * gate) * y, g_ref[...], b_ref[...])


def _ffn(h, mod3, wg, wu, wd, g, b, n_tiles, mod_index):
    return pl.pallas_call(
        _ffn_kernel,
        grid=(n_tiles,),
        in_specs=[pl.BlockSpec((TM, D_MODEL), lambda i: (i, 0)),
                  pl.BlockSpec((1, 3, D_MODEL), lambda i: (mod_index(i), 0, 0)),
                  _resident((D_MODEL, D_FF), lambda i: (0, 0)),
                  _resident((D_MODEL, D_FF), lambda i: (0, 0)),
                  _resident((D_FF, D_MODEL), lambda i: (0, 0)),
                  pl.BlockSpec((1, D_MODEL), lambda i: (0, 0)),
                  pl.BlockSpec((1, D_MODEL), lambda i: (0, 0))],
        out_specs=pl.BlockSpec((TM, D_MODEL), lambda i: (i, 0)),
        out_shape=jax.ShapeDtypeStruct((n_tiles * TM, D_MODEL), F32),
        compiler_params=_cparams(1),
        name="ffn_postnorm",
    )(h, mod3, wg, wu, wd, g, b)


def _ones_rows(n):
    row = lax.broadcasted_iota(jnp.int32, (BF16_ROWS, n), 0)
    return jnp.where(row == 0, 1.0, 0.0).astype(BF16)


def _rope(x, cos, sin_signed, quarter):
    lane = lax.broadcasted_iota(jnp.int32, x.shape, 1)
    first = (lane % (2 * quarter)) < quarter
    rot = jnp.where(first, pltpu.roll(x, LANES - quarter, 1), pltpu.roll(x, quarter, 1))
    return x * cos + rot * sin_signed


EV_QLAT = 0
EV_BQ = 256
EV_KVLAT = 768
EV_KPE = 896
EV_BK = 1024
EV_BV = 1536
EV_PCOLS = 2048


def _even_proj_kernel(h_ref, mod_ref, win_ref, gq_ref, gkv_ref, wuq_ref, wabs_ref,
                      cosa_ref, sina_ref, cosb_ref, sinb_ref,
                      qa_ref, ka_ref, ct_ref, qb_ref, kb_ref, vbt_ref):
    h = h_ref[...]
    hm = (h * (1.0 + mod_ref[0, 1:2, :]) + mod_ref[0, 0:1, :]).astype(BF16)
    p = _dot(hm, win_ref[...])
    cosa, sina, cosb, sinb = cosa_ref[...], sina_ref[...], cosb_ref[...], sinb_ref[...]

    qn = _rms_norm(p[:, EV_QLAT:EV_QLAT + A_Q_RANK], gq_ref[...], RMS_EPS).astype(BF16)
    q = _dot(qn, wuq_ref[...])
    n_nope = A_HEADS * A_NOPE
    qabs = _dot(q[:, :n_nope].astype(BF16), wabs_ref[...])
    qpe = [_rope(q[:, n_nope + LANES * s:n_nope + LANES * (s + 1)], cosa, sina, A_ROPE // 4) for s in range(2)]
    qpe_t = jnp.concatenate(qpe, axis=1).T
    for hd in range(A_HEADS):
        qa_ref[hd, 0:A_KV_RANK, :] = (qabs[:, A_KV_RANK * hd:A_KV_RANK * (hd + 1)].T * A_QSCALE).astype(BF16)
        qa_ref[hd, A_KV_RANK:A_QK, :] = (qpe_t[A_ROPE * hd:A_ROPE * (hd + 1), :] * A_QSCALE).astype(BF16)

    cn = _rms_norm(p[:, EV_KVLAT:EV_KVLAT + A_KV_RANK], gkv_ref[...], RMS_EPS)
    kpe = _rope(p[:, EV_KPE:EV_KPE + LANES], cosa, sina, A_ROPE // 4)
    ka_ref[:, 0:A_KV_RANK] = cn.astype(BF16)
    ka_ref[:, A_KV_RANK:A_QK] = kpe[:, 0:A_ROPE].astype(BF16)
    ones_rows = _ones_rows(TM)
    cnt = jnp.concatenate([cn.T.astype(BF16), ones_rows], axis=0)
    for s in range(TM // VB):
        ct_ref[s] = cnt[:, VB * s:VB * (s + 1)]

    for hd in range(B_HEADS):
        sl = slice(LANES * hd, LANES * (hd + 1))
        qb_ref[hd] = (_rope(p[:, EV_BQ:EV_BQ + B_QK][:, sl], cosb, sinb, B_HD // 4).T * B_QSCALE).astype(BF16)
        kb_ref[hd] = _rope(p[:, EV_BK:EV_BK + B_QK][:, sl], cosb, sinb, B_HD // 4).astype(BF16)
        vt = jnp.concatenate([p[:, EV_BV:EV_BV + B_QK][:, sl].T.astype(BF16), ones_rows], axis=0)
        for s in range(TM // VB):
            vbt_ref[hd, s] = vt[:, VB * s:VB * (s + 1)]


def _even_proj(h, mod3, win, gq, gkv, wuq, wabs, tabs, n_tiles, mod_index, tab_index):
    t_all = n_tiles * TM
    row = lambda i: (i, 0)
    tab = lambda i: (tab_index(i), 0)
    const = lambda i: (0, 0)
    return pl.pallas_call(
        _even_proj_kernel,
        grid=(n_tiles,),
        in_specs=[pl.BlockSpec((TM, D_MODEL), row),
                  pl.BlockSpec((1, 3, D_MODEL), lambda i: (mod_index(i), 0, 0)),
                  _resident((D_MODEL, EV_PCOLS), const),
                  pl.BlockSpec((1, A_Q_RANK), const),
                  pl.BlockSpec((1, A_KV_RANK), const),
                  _resident((A_Q_RANK, A_HEADS * (A_NOPE + A_ROPE)), const),
                  _resident((A_HEADS * A_NOPE, A_HEADS * A_KV_RANK), const),
                  pl.BlockSpec((TM, LANES), tab), pl.BlockSpec((TM, LANES), tab),
                  pl.BlockSpec((TM, LANES), tab), pl.BlockSpec((TM, LANES), tab)],
        out_specs=[pl.BlockSpec((A_HEADS, A_QK, TM), lambda i: (0, 0, i)),
                   pl.BlockSpec((TM, A_QK), row),
                   pl.BlockSpec((TM // VB, DV, VB), lambda i: (i, 0, 0)),
                   pl.BlockSpec((B_HEADS, LANES, TM), lambda i: (0, 0, i)),
                   pl.BlockSpec((B_HEADS, TM, LANES), lambda i: (0, i, 0)),
                   pl.BlockSpec((B_HEADS, TM // VB, DV, VB), lambda i: (0, i, 0, 0))],
        out_shape=[jax.ShapeDtypeStruct((A_HEADS, A_QK, t_all), BF16),
                   jax.ShapeDtypeStruct((t_all, A_QK), BF16),
                   jax.ShapeDtypeStruct((t_all // VB, DV, VB), BF16),
                   jax.ShapeDtypeStruct((B_HEADS, LANES, t_all), BF16),
                   jax.ShapeDtypeStruct((B_HEADS, t_all, LANES), BF16),
                   jax.ShapeDtypeStruct((B_HEADS, t_all // VB, DV, VB), BF16)],
        compiler_params=_cparams(1),
        name="even_proj",
    )(h, mod3, win, gq, gkv, wuq, wabs, *tabs)


N_SLOTS = 8
KEY_GROUP = 2


class _Flash:
    def __init__(self, s_refs, p_refs, acc_ref):
        self.s, self.p, self.acc = s_refs, p_refs, acc_ref

    def qk(self, b, k, q_t):
        rows = k.shape[0]
        s = _dot(k, q_t)
        self.s[b % N_SLOTS][0:rows, :] = s
        return jnp.max(s.reshape(rows // SUBLANES, SUBLANES, NQ), axis=0)

    def sm(self, blocks, rows, maxima, m_old):
        m8 = maxima[0]
        for x in maxima[1:]:
            m8 = jnp.maximum(m8, x)
        m_new = jnp.maximum(m_old, jnp.max(m8, axis=0, keepdims=True))
        alpha = jnp.exp2(m_old - m_new)
        for b in blocks:
            s_ref, p_ref = self.s[b % N_SLOTS], self.p[b % N_SLOTS]
            for r in range(rows // STRIP):
                p_ref[STRIP * r:STRIP * (r + 1), :] = jnp.exp2(s_ref[STRIP * r:STRIP * (r + 1), :] - m_new).astype(BF16)
        return m_new, alpha

    def pv(self, blocks, vt_lists, alpha):
        pv = None
        for b, vts in zip(blocks, vt_lists):
            for i, vt in enumerate(vts):
                d = _dot(vt, self.p[b % N_SLOTS][VB * i:VB * (i + 1), :])
                pv = d if pv is None else pv + d
        self.acc[...] = pv if alpha is None else alpha * self.acc[...] + pv

    def run(self, q_t, key_blocks, group):
        groups, i = [], 0
        while i < len(key_blocks):
            j = i + 1
            while j < len(key_blocks) and j - i < group and key_blocks[j][0].shape == key_blocks[i][0].shape:
                j += 1
            groups.append(list(range(i, j)))
            i = j
        m = jnp.full((1, NQ), NEG_INF, F32)
        maxima, alphas = {}, {}
        for h in range(len(groups) + 2):
            if h < len(groups):
                for b in groups[h]:
                    maxima[b] = self.qk(b, key_blocks[b][0], q_t)
            if 0 <= h - 1 < len(groups):
                g = groups[h - 1]
                m, alphas[h - 1] = self.sm(g, key_blocks[g[0]][0].shape[0], [maxima[b] for b in g], m)
            if 0 <= h - 2 < len(groups):
                g = groups[h - 2]
                self.pv(g, [key_blocks[b][1] for b in g], alphas[h - 2] if h > 2 else None)

    def normalised(self):
        return self.acc[0:LANES, :] * (1.0 / self.acc[LANES:LANES + 1, :])


def _flash_scratch():
    return ([pltpu.VMEM((TK, NQ), F32)] * N_SLOTS + [pltpu.VMEM((TK, NQ), BF16)] * N_SLOTS
            + [pltpu.VMEM((DV, NQ), F32)])


def _flash_from_scratch(refs):
    return _Flash(refs[0:N_SLOTS], refs[N_SLOTS:2 * N_SLOTS], refs[2 * N_SLOTS])


def _block_diag_queries_t(q_t):
    row = lax.broadcasted_iota(jnp.int32, q_t.shape, 0)
    zero = jnp.zeros_like(q_t)
    half = LANES // 2
    return jnp.concatenate([jnp.where(row < half, q_t, zero), jnp.where(row >= half, q_t, zero)], axis=1)


def _block_diag_queries(q12):
    lane = lax.broadcasted_iota(jnp.int32, q12.shape, 1)
    zero = jnp.zeros_like(q12)
    half = LANES // 2
    return jnp.concatenate([jnp.where(lane < half, q12, zero), jnp.where(lane >= half, q12, zero)], axis=0)


def _mla_kernel(qa_ref, kl_ref, kc_ref, ctl_ref, ctc_ref, wuvt_ref, o_ref, *scratch, n_lat_steps, n_qb_lat):
    fl = _flash_from_scratch(scratch)
    is_latent = pl.program_id(1) < n_qb_lat
    ctx_block = lambda: (kc_ref[...], [ctc_ref[0]])
    lat_block = lambda t: (kl_ref[TK * t:TK * (t + 1), :], [ctl_ref[(TK // VB) * t + i] for i in range(TK // VB)])

    def pair(g, carry):
        q_t = jnp.concatenate([qa_ref[2 * g], qa_ref[2 * g + 1]], axis=1)

        @pl.when(is_latent)
        def _():
            fl.run(q_t, [lat_block(t) for t in range(n_lat_steps)] + [ctx_block()], KEY_GROUP)

        @pl.when(jnp.logical_not(is_latent))
        def _():
            fl.run(q_t, [ctx_block()], KEY_GROUP)

        o_lat = fl.normalised().astype(BF16)
        for i in range(2):
            o = _dot(wuvt_ref[2 * g + i], o_lat[:, TQ * i:TQ * (i + 1)])
            o_ref[pl.ds(pl.multiple_of((2 * g + i) * A_V, A_V), A_V), :] = o.astype(BF16)
        return carry

    lax.fori_loop(0, A_HEADS // 2, pair, 0)


def _mla(qa, ka, ct, wuvt, bsz, n, m):
    t_lat = bsz * n
    t_all = t_lat + bsz * m
    nqb = n // TQ
    qblk = lambda b, j: jnp.where(j < nqb, b * nqb + j, t_lat // TQ + b)
    kern = functools.partial(_mla_kernel, n_lat_steps=n // TK, n_qb_lat=nqb)
    return pl.pallas_call(
        kern,
        grid=(bsz, nqb + 1),
        in_specs=[pl.BlockSpec((A_HEADS, A_QK, TQ), lambda b, j: (0, 0, qblk(b, j))),
                  _resident((n, A_QK), lambda b, j: (b, 0)),
                  pl.BlockSpec((m, A_QK), lambda b, j: (t_lat // m + b, 0)),
                  _resident((n // VB, DV, VB), lambda b, j: (b, 0, 0)),
                  pl.BlockSpec((m // VB, DV, VB), lambda b, j: (t_lat // m + b, 0, 0)),
                  pl.BlockSpec((A_HEADS, A_V, A_KV_RANK), lambda b, j: (0, 0, 0))],
        out_specs=pl.BlockSpec((A_HEADS * A_V, TQ), lambda b, j: (0, qblk(b, j))),
        out_shape=jax.ShapeDtypeStruct((A_HEADS * A_V, t_all), BF16),
        scratch_shapes=_flash_scratch(),
        compiler_params=_cparams(2),
        name="mla_attention",
    )(qa, ka, ka, ct, ct, wuvt)


def _diff_kernel(qb_ref, kl_ref, kc_ref, vl_ref, vc_ref, lam_ref, gsub_ref, o_ref, *scratch, n_lat_steps, n_qb_lat,
                 lam_init):
    fl = _flash_from_scratch(scratch)
    is_latent = pl.program_id(1) < n_qb_lat
    lv = lam_ref[...]
    lam = (jnp.exp(jnp.sum(lv[0:1] * lv[1:2], axis=1, keepdims=True))
           - jnp.exp(jnp.sum(lv[2:3] * lv[3:4], axis=1, keepdims=True)) + lam_init)

    def head(hd, carry):
        q_t = _block_diag_queries_t(qb_ref[hd])
        ctx_block = lambda: (kc_ref[hd], [vc_ref[hd, 0]])
        lat_block = lambda t: (kl_ref[hd, TK * t:TK * (t + 1), :],
                               [vl_ref[hd, (TK // VB) * t + i] for i in range(TK // VB)])

        @pl.when(is_latent)
        def _():
            fl.run(q_t, [lat_block(t) for t in range(n_lat_steps)] + [ctx_block()], KEY_GROUP)

        @pl.when(jnp.logical_not(is_latent))
        def _():
            fl.run(q_t, [ctx_block()], KEY_GROUP)

        on = fl.normalised()
        o = on[:, 0:TQ] - lam * on[:, TQ:2 * TQ]
        o = o * lax.rsqrt(jnp.mean(o * o, axis=0, keepdims=True) + SUBLN_EPS) * gsub_ref[...]
        o_ref[pl.ds(pl.multiple_of(hd * LANES, LANES), LANES), :] = (o * (1.0 - lam_init)).astype(BF16)
        return carry

    lax.fori_loop(0, B_HEADS, head, 0)


def _diff(qb, kb, vbt, lam_vec, gsub_col, lam_init, bsz, n, m):
    t_lat = bsz * n
    t_all = t_lat + bsz * m
    nqb = n // TQ
    qblk = lambda b, j: jnp.where(j < nqb, b * nqb + j, t_lat // TQ + b)
    kern = functools.partial(_diff_kernel, n_lat_steps=n // TK, n_qb_lat=nqb, lam_init=lam_init)
    return pl.pallas_call(
        kern,
        grid=(bsz, nqb + 1),
        in_specs=[pl.BlockSpec((B_HEADS, LANES, TQ), lambda b, j: (0, 0, qblk(b, j))),
                  _resident((B_HEADS, n, LANES), lambda b, j: (0, b, 0)),
                  pl.BlockSpec((B_HEADS, m, LANES), lambda b, j: (0, t_lat // m + b, 0)),
                  _resident((B_HEADS, n // VB, DV, VB), lambda b, j: (0, b, 0, 0)),
                  pl.BlockSpec((B_HEADS, m // VB, DV, VB), lambda b, j: (0, t_lat // m + b, 0, 0)),
                  pl.BlockSpec((4, B_HD), lambda b, j: (0, 0)),
                  pl.BlockSpec((2 * B_HD, 1), lambda b, j: (0, 0))],
        out_specs=pl.BlockSpec((B_HEADS * LANES, TQ), lambda b, j: (0, qblk(b, j))),
        out_shape=jax.ShapeDtypeStruct((B_HEADS * LANES, t_all), BF16),
        scratch_shapes=_flash_scratch(),
        compiler_params=_cparams(2),
        name="diff_attention",
    )(qb, kb, kb, vbt, vbt, lam_vec, gsub_col)


def _outproj_kernel(h_ref, mod_ref, a_ref, b_ref, w_ref, g_ref, beta_ref, o_ref, *, a_transposed):
    half = w_ref.shape[0] // 2
    ya = _dot_tn(a_ref[...], w_ref[0:half, :]) if a_transposed else _dot(a_ref[...], w_ref[0:half, :])
    y = ya + _dot_tn(b_ref[...], w_ref[half:2 * half, :])
    z = DN_ALPHA * h_ref[...] + mod_ref[0, 2:3, :] * y
    o_ref[...] = _layer_norm(z, g_ref[...], beta_ref[...])


def _outproj(h, mod3, mix_a, mix_b, w, g, b, n_tiles, mod_index, a_transposed):
    half = w.shape[0] // 2
    a_spec = (pl.BlockSpec((half, TM), lambda i: (0, i)) if a_transposed
              else pl.BlockSpec((TM, half), lambda i: (i, 0)))
    return pl.pallas_call(
        functools.partial(_outproj_kernel, a_transposed=a_transposed),
        grid=(n_tiles,),
        in_specs=[pl.BlockSpec((TM, D_MODEL), lambda i: (i, 0)),
                  pl.BlockSpec((1, 3, D_MODEL), lambda i: (mod_index(i), 0, 0)),
                  a_spec,
                  pl.BlockSpec((half, TM), lambda i: (0, i)),
                  _resident((2 * half, D_MODEL), lambda i: (0, 0)),
                  pl.BlockSpec((1, D_MODEL), lambda i: (0, 0)),
                  pl.BlockSpec((1, D_MODEL), lambda i: (0, 0))],
        out_specs=pl.BlockSpec((TM, D_MODEL), lambda i: (i, 0)),
        out_shape=jax.ShapeDtypeStruct((n_tiles * TM, D_MODEL), F32),
        compiler_params=_cparams(1),
        name="outproj_postnorm",
    )(h, mod3, mix_a, mix_b, w, g, b)


def _odd_proj_kernel(h_ref, mod_ref, win_ref, u_ref, q_ref, k_ref, vt_ref):
    h = h_ref[...]
    hm = (h * (1.0 + mod_ref[0, 1:2, :]) + mod_ref[0, 0:1, :]).astype(BF16)
    p = _dot(hm, win_ref[...])
    u_ref[...] = p[:, 0:C_WIDTH]
    for s in range(D_WIDTH // LANES):
        sl = slice(LANES * s, LANES * (s + 1))
        q_ref[s] = (p[:, C_WIDTH:C_WIDTH + D_WIDTH][:, sl] * D_SCALE).astype(BF16)
        k_ref[s] = p[:, C_WIDTH + D_WIDTH:C_WIDTH + 2 * D_WIDTH][:, sl].astype(BF16)
        vt = p[:, C_WIDTH + 2 * D_WIDTH:C_WIDTH + 3 * D_WIDTH][:, sl].T.astype(BF16)
        for i in range(TM // VB):
            vt_ref[s, i] = vt[:, VB * i:VB * (i + 1)]


def _odd_proj(h, mod3, win, n_tiles, mod_index):
    t_all = n_tiles * TM
    slabs = D_WIDTH // LANES
    return pl.pallas_call(
        _odd_proj_kernel,
        grid=(n_tiles,),
        in_specs=[pl.BlockSpec((TM, D_MODEL), lambda i: (i, 0)),
                  pl.BlockSpec((1, 3, D_MODEL), lambda i: (mod_index(i), 0, 0)),
                  _resident((D_MODEL, C_WIDTH + 3 * D_WIDTH), lambda i: (0, 0))],
        out_specs=[pl.BlockSpec((TM, C_WIDTH), lambda i: (i, 0)),
                   pl.BlockSpec((slabs, TM, LANES), lambda i: (0, i, 0)),
                   pl.BlockSpec((slabs, TM, LANES), lambda i: (0, i, 0)),
                   pl.BlockSpec((slabs, TM // VB, LANES, VB), lambda i: (0, i, 0, 0))],
        out_shape=[jax.ShapeDtypeStruct((t_all, C_WIDTH), F32),
                   jax.ShapeDtypeStruct((slabs, t_all, LANES), BF16),
                   jax.ShapeDtypeStruct((slabs, t_all, LANES), BF16),
                   jax.ShapeDtypeStruct((slabs, t_all // VB, LANES, VB), BF16)],
        compiler_params=_cparams(1),
        name="odd_proj",
    )(h, mod3, win)


def _pool_kernel(prev_ref, u_ref, next_ref, w_ref, ps_ref, o_ref, ext_ref, *, tiles_per_seq):
    t = pl.program_id(0) % tiles_per_seq
    n = tiles_per_seq * TM
    ext_ref[0:POOL_HALO, :] = jnp.where(t > 0, prev_ref[...], 0.0)
    ext_ref[POOL_HALO:POOL_HALO + TM, :] = u_ref[...]
    ext_ref[POOL_HALO + TM:2 * POOL_HALO + TM, :] = jnp.where(t < tiles_per_seq - 1, next_ref[...], 0.0)
    pos = t * TM + lax.broadcasted_iota(jnp.int32, (TM, 1), 0)
    for g, w in enumerate(C_WINDOWS):
        left = w // 2
        right = w - 1 - left
        lanes = slice(C_GW * g, C_GW * (g + 1))
        tot = ext_ref[POOL_HALO - left:POOL_HALO - left + TM, lanes]
        for k in range(1 - left, right + 1):
            tot = tot + ext_ref[POOL_HALO + k:POOL_HALO + k + TM, lanes]
        cnt = jnp.minimum(pos + right + 1, n) - jnp.maximum(pos - left, 0)
        pooled = tot / cnt.astype(F32) - u_ref[:, lanes]
        y = _dot(pooled.astype(BF16), w_ref[g])
        o_ref[:, lanes] = (y * ps_ref[:, lanes]).astype(BF16)


def _pool(u, w_pool, pool_scale, n_tiles, tiles_per_seq):
    per = TM // POOL_HALO
    last = u.shape[0] // POOL_HALO - 1
    return pl.pallas_call(
        functools.partial(_pool_kernel, tiles_per_seq=tiles_per_seq),
        grid=(n_tiles,),
        in_specs=[pl.BlockSpec((POOL_HALO, C_WIDTH), lambda i: (jnp.maximum(i * per - 1, 0), 0)),
                  pl.BlockSpec((TM, C_WIDTH), lambda i: (i, 0)),
                  pl.BlockSpec((POOL_HALO, C_WIDTH), lambda i: (jnp.minimum((i + 1) * per, last), 0)),
                  pl.BlockSpec((C_GROUPS, C_GW, C_GW), lambda i: (0, 0, 0)),
                  pl.BlockSpec((1, C_WIDTH), lambda i: (0, 0))],
        out_specs=pl.BlockSpec((TM, C_WIDTH), lambda i: (i, 0)),
        out_shape=jax.ShapeDtypeStruct((n_tiles * TM, C_WIDTH), BF16),
        scratch_shapes=[pltpu.VMEM((TM + 2 * POOL_HALO, C_WIDTH), F32)],
        compiler_params=_cparams(1),
        name="multiscale_pool",
    )(u, u, u, w_pool, pool_scale)


def _rpb_kernel(rpb_ref, o_ref):
    hd = pl.program_id(0)
    n_r, n_c = 2 * NA_ROWS - 1, 2 * NA_COLS - 1
    idx = (lax.broadcasted_iota(jnp.int32, (GRID_W, GRID_W), 0)
           - lax.broadcasted_iota(jnp.int32, (GRID_W, GRID_W), 1) + NA_COLS - 1)
    for dr in range(n_r):
        acc = jnp.zeros((GRID_W, GRID_W), F32)
        for j in range(n_c):
            acc = jnp.where(idx == j, rpb_ref[hd * n_r * n_c + dr * n_c + j], acc)
        o_ref[0, dr] = acc


def _rpb_blocks(rpb):
    n_r, n_c = 2 * NA_ROWS - 1, 2 * NA_COLS - 1
    return pl.pallas_call(
        _rpb_kernel,
        grid=(D_HEADS,),
        in_specs=[pl.BlockSpec(memory_space=pltpu.SMEM)],
        out_specs=pl.BlockSpec((1, n_r, GRID_W, GRID_W), lambda i: (i, 0, 0, 0)),
        out_shape=jax.ShapeDtypeStruct((D_HEADS, n_r, GRID_W, GRID_W), F32),
        compiler_params=_cparams(1),
        name="rpb_blocks",
    )(rpb.reshape(D_HEADS * n_r * n_c))


def _na_bias_layout(blocks, rows):
    qr_per = TQ // GRID_W
    g_of = (0, 1, rows // qr_per - 1)
    gb_of = tuple(min(max(g - 1, 0), rows // qr_per - 3) for g in g_of)
    ridx = np.zeros((NA_PATTERNS, NA_KEY_ROWS, qr_per), np.int32)
    rvalid = np.zeros((NA_PATTERNS, NA_KEY_ROWS, qr_per), bool)
    for p in range(NA_PATTERNS):
        for kr in range(NA_KEY_ROWS):
            for qr in range(qr_per):
                r = qr_per * g_of[p] + qr
                ka = qr_per * gb_of[p] + kr
                r0 = min(max(r - NA_ROWS // 2, 0), rows - NA_ROWS)
                rvalid[p, kr, qr] = r0 <= ka < r0 + NA_ROWS
                ridx[p, kr, qr] = min(max(ka - r + NA_ROWS - 1, 0), 2 * NA_ROWS - 2)
    j = np.arange(GRID_W)
    cs = np.clip(j - NA_COLS // 2, 0, GRID_W - NA_COLS)
    cvalid = (j[:, None] >= cs[None, :]) & (j[:, None] < cs[None, :] + NA_COLS)
    valid = rvalid[:, :, None, :, None] & cvalid[None, None, :, None, :]
    t = blocks[:, ridx]
    t = jnp.transpose(t, (1, 0, 2, 4, 3, 5))
    t = jnp.where(valid[:, None], t, NEG_INF)
    t = t.reshape(NA_PATTERNS, D_HEADS // 2, 2, NA_KEY_ROWS * GRID_W, TQ)
    t = jnp.transpose(t, (0, 1, 3, 2, 4))
    return t.reshape(NA_PATTERNS, D_HEADS // 2, NA_KEY_ROWS * GRID_W, 2 * TQ)


def _na_kernel(q_ref, kl_ref, kc_ref, vl_ref, vc_ref, bias_ref, o_ref, *, nqb):
    g = pl.program_id(2)
    gb = jnp.clip(g - 1, 0, nqb - 3)
    n_nb = NA_KEY_ROWS * GRID_W
    qbd = _block_diag_queries(q_ref[0])
    k_nb = kl_ref[0, pl.ds(pl.multiple_of(gb * TQ, TQ), n_nb), :]
    s_nb = _dot_nt(k_nb, qbd) + bias_ref[0, 0]
    s_cx = _dot_nt(kc_ref[0], qbd)
    mx = jnp.maximum(jnp.max(s_nb, axis=0, keepdims=True), jnp.max(s_cx, axis=0, keepdims=True))
    p_nb = jnp.exp(s_nb - mx)
    p_cx = jnp.exp(s_cx - mx)
    l = jnp.sum(p_nb, axis=0, keepdims=True) + jnp.sum(p_cx, axis=0, keepdims=True)
    pb = p_nb.astype(BF16)
    acc = _dot(vc_ref[0, 0], p_cx.astype(BF16))
    for i in range(n_nb // VB):
        acc = acc + _dot(vl_ref[0, gb + i], pb[VB * i:VB * (i + 1)])
    on = acc * (1.0 / l)
    half = LANES // 2
    o_ref[0:half, :] = on[0:half, 0:TQ].astype(BF16)
    o_ref[half:LANES, :] = on[half:LANES, TQ:2 * TQ].astype(BF16)


def _natten(qn, kn, vnt, bias, bsz, n, m):
    t_lat = bsz * n
    nqb = n // TQ
    slabs = D_WIDTH // LANES
    pat = lambda g: jnp.where(g == 0, 0, jnp.where(g == nqb - 1, 2, 1))
    return pl.pallas_call(
        functools.partial(_na_kernel, nqb=nqb),
        grid=(bsz, slabs, nqb),
        in_specs=[pl.BlockSpec((1, TQ, LANES), lambda b, s, g: (s, b * nqb + g, 0)),
                  pl.BlockSpec((1, n, LANES), lambda b, s, g: (s, b, 0)),
                  pl.BlockSpec((1, m, LANES), lambda b, s, g: (s, t_lat // m + b, 0)),
                  pl.BlockSpec((1, n // VB, LANES, VB), lambda b, s, g: (s, b, 0, 0)),
                  pl.BlockSpec((1, m // VB, LANES, VB), lambda b, s, g: (s, t_lat // m + b, 0, 0)),
                  pl.BlockSpec((1, 1, NA_KEY_ROWS * GRID_W, 2 * TQ), lambda b, s, g: (pat(g), s, 0, 0))],
        out_specs=pl.BlockSpec((LANES, TQ), lambda b, s, g: (s, b * nqb + g)),
        out_shape=jax.ShapeDtypeStruct((D_WIDTH, t_lat), BF16),
        compiler_params=_cparams(3),
        name="neighbourhood_attention",
    )(qn, kn, kn, vnt, vnt, bias)


def _rope_tables(n, rot_dim):
    t = np.arange(n)
    row = (t // GRID_W).astype(np.float32)
    col = (t % GRID_W).astype(np.float32)
    quarter = rot_dim // 4
    inv = jnp.asarray(ROPE_BASE, F32) ** (-jnp.arange(quarter, dtype=F32) / quarter)
    ang_r = jnp.asarray(row)[:, None] * inv[None, :]
    ang_c = jnp.asarray(col)[:, None] * inv[None, :]
    ang = jnp.concatenate([ang_r, ang_r, ang_c, ang_c], -1)
    sign = np.where((np.arange(rot_dim) % (2 * quarter)) < quarter, -1.0, 1.0).astype(np.float32)
    cos = jnp.tile(jnp.cos(ang), (1, LANES // rot_dim))
    sin = jnp.tile(jnp.sin(ang) * sign[None, :], (1, LANES // rot_dim))
    cos = jnp.concatenate([cos, jnp.ones((TM, LANES), F32)], 0)
    sin = jnp.concatenate([sin, jnp.zeros((TM, LANES), F32)], 0)
    return cos, sin


def _even_weights(w_in, w_uq, w_ukv):
    o = A_Q_RANK + B_QK
    kpe = jnp.pad(w_in[:, o + A_KV_RANK:o + A_KV_RANK + A_ROPE], ((0, 0), (0, LANES - A_ROPE)))
    win = jnp.concatenate([w_in[:, 0:A_Q_RANK], w_in[:, A_Q_RANK:o], w_in[:, o:o + A_KV_RANK], kpe,
                           w_in[:, o + A_KV_RANK + A_ROPE:o + A_KV_RANK + A_ROPE + B_QK],
                           w_in[:, o + A_KV_RANK + A_ROPE + B_QK:]], axis=1).astype(BF16)
    uq = w_uq.reshape(A_Q_RANK, A_HEADS, A_NOPE + A_ROPE)
    wuq = jnp.concatenate([uq[:, :, :A_NOPE].reshape(A_Q_RANK, A_HEADS * A_NOPE),
                           uq[:, :, A_NOPE:].reshape(A_Q_RANK, A_HEADS * A_ROPE)], axis=1).astype(BF16)
    ukv = w_ukv.reshape(A_KV_RANK, A_HEADS, A_NOPE + A_V)
    uk_t = jnp.transpose(ukv[:, :, :A_NOPE], (1, 2, 0))
    wabs = jnp.zeros((A_HEADS, A_NOPE, A_HEADS, A_KV_RANK), F32)
    wabs = wabs.at[np.arange(A_HEADS), :, np.arange(A_HEADS), :].set(uk_t)
    wabs = wabs.reshape(A_HEADS * A_NOPE, A_HEADS * A_KV_RANK).astype(BF16)
    wuvt = jnp.transpose(ukv[:, :, A_NOPE:], (1, 2, 0)).astype(BF16)
    return win, wuq, wabs, wuvt


def kernel(x, c, ctx, c_ctx, ada_w, ada_b, ln_g, ln_b, ffn_w_gate, ffn_w_up, ffn_w_down, ev_w_in, ev_w_out,
           ev_g_qlat, ev_g_kvlat, ev_w_uq, ev_w_ukv, ev_lam, ev_g_sub, od_w_in, od_w_out, od_w_pool,
           od_pool_scale, od_rpb):
    bsz, n, d = x.shape
    m = ctx.shape[1]
    assert d == D_MODEL and ada_w.shape[0] == DEPTH == 2
    assert m == TQ and n % TK == 0 and n % GRID_W == 0 and (bsz * m) % TM == 0 and bsz + 1 <= 8
    assert n // TQ >= 3
    t_lat = bsz * n
    t_all = t_lat + bsz * m
    tiles_lat, tiles_all, tiles_seq = t_lat // TM, t_all // TM, n // TM
    mod_index = lambda i: jnp.minimum(i // tiles_seq, bsz)
    tab_index = lambda i: jnp.where(i < tiles_lat, i % tiles_seq, tiles_seq)

    c_all = jnp.zeros((8, D_MODEL), F32).at[:bsz].set(c).at[bsz].set(c_ctx)
    mod = _modulation(c_all, ada_w, ada_b).reshape(DEPTH, 8, N_MOD, D_MODEL)
    h = jnp.concatenate([x.reshape(t_lat, D_MODEL), ctx.reshape(bsz * m, D_MODEL)], axis=0)
    row = lambda v: v.reshape(1, -1)
    wg, wu, wd = ffn_w_gate.astype(BF16), ffn_w_up.astype(BF16), ffn_w_down.astype(BF16)

    h = _ffn(h, mod[0, :, 0:3], wg[0, 0], wu[0, 0], wd[0, 0], row(ln_g[0, 0]), row(ln_b[0, 0]), tiles_all, mod_index)
    win, wuq, wabs, wuvt = _even_weights(ev_w_in[0], ev_w_uq[0], ev_w_ukv[0])
    tabs = _rope_tables(n, A_ROPE) + _rope_tables(n, B_HD)
    qa, ka, ct, qb, kb, vbt = _even_proj(h, mod[0, :, 3:6], win, row(ev_g_qlat[0]), row(ev_g_kvlat[0]), wuq, wabs,
                                         tabs, tiles_all, mod_index, tab_index)
    lam_init = 0.8 - 0.6 * math.exp(-0.3 * 0)
    mix_a = _mla(qa, ka, ct, wuvt, bsz, n, m)
    mix_b = _diff(qb, kb, vbt, ev_lam[0], ev_g_sub[0].reshape(2 * B_HD, 1), lam_init, bsz, n, m)
    h = _outproj(h, mod[0, :, 3:6], mix_a, mix_b, ev_w_out[0].astype(BF16), row(ln_g[0, 1]), row(ln_b[0, 1]),
                 tiles_all, mod_index, True)
    h = _ffn(h, mod[0, :, 6:9], wg[0, 1], wu[0, 1], wd[0, 1], row(ln_g[0, 2]), row(ln_b[0, 2]), tiles_all, mod_index)

    h = _ffn(h, mod[1, :, 0:3], wg[1, 0], wu[1, 0], wd[1, 0], row(ln_g[1, 0]), row(ln_b[1, 0]), tiles_all, mod_index)
    u, qn, kn, vnt = _odd_proj(h, mod[1, :, 3:6], od_w_in[0].astype(BF16), tiles_all, mod_index)
    y_pool = _pool(u, od_w_pool[0].astype(BF16), row(od_pool_scale[0]), tiles_lat, tiles_seq)
    bias = _na_bias_layout(_rpb_blocks(od_rpb[0]), n // GRID_W)
    y_na = _natten(qn, kn, vnt, bias, bsz, n, m)
    h = _outproj(h, mod[1, :, 3:6], y_pool, y_na, od_w_out[0].astype(BF16), row(ln_g[1, 1]), row(ln_b[1, 1]),
                 tiles_lat, mod_index, False)
    h = _ffn(h, mod[1, :, 6:9], wg[1, 1], wu[1, 1], wd[1, 1], row(ln_g[1, 2]), row(ln_b[1, 2]), tiles_lat, mod_index)
    return h.reshape(bsz, n, D_MODEL)
```

```python
import functools
import math

import numpy as np
import jax
import jax.numpy as jnp
from jax import lax
from jax.experimental import pallas as pl
from jax.experimental.pallas import tpu as pltpu

F32 = jnp.float32
BF16 = jnp.bfloat16

D_MODEL = 1024
DEPTH = 2
GRID_W = 64
N_MOD = 9
DN_ALPHA = float((2 * DEPTH) ** 0.25)
LN_EPS = 1e-6
RMS_EPS = 1e-6
SUBLN_EPS = 1e-5
ROPE_BASE = 10000.0
NEG_INF = -1e30
D_FF = 2816

A_HEADS = 8
A_NOPE = 64
A_ROPE = 32
A_V = 64
A_Q_RANK = 256
A_KV_RANK = 128
A_SCALE = float((A_NOPE + A_ROPE) ** -0.5)
A_QK = A_KV_RANK + A_ROPE

B_HEADS = 4
B_HD = 64
B_SCALE = float(B_HD ** -0.5)
B_QK = B_HEADS * 2 * B_HD

C_GROUPS = 4
C_WINDOWS = (2, 4, 8, 16)
C_GW = 128
C_WIDTH = C_GROUPS * C_GW

D_HEADS = 8
D_HD = 64
D_SCALE = float(D_HD ** -0.5)
D_WIDTH = D_HEADS * D_HD
NA_ROWS = 8
NA_COLS = 16

LOG2E = math.log2(math.e)
A_QSCALE = A_SCALE * LOG2E
B_QSCALE = B_SCALE * LOG2E
D_QSCALE = D_SCALE * LOG2E

LANES = 128
SUBLANES = 8
TM = 512
FFN_ROWS = 2 * TM
TQ = 256
NQ = 2 * TQ
TK = 512
VB = 256
STRIP = 64
BF16_ROWS = 16
DV = LANES + BF16_ROWS
MXU_DIM = 256
FF_SPLITS = (0, 6 * MXU_DIM, D_FF)
POOL_HALO = 8
NA_KEY_ROWS = 12
NA_PATTERNS = 3
NA_GS = 4
VMEM_LIMIT = 56 * 1024 * 1024


def _cparams(n_axes):
    return pltpu.CompilerParams(dimension_semantics=("arbitrary",) * n_axes,
                                vmem_limit_bytes=VMEM_LIMIT)


def _resident(shape, index_map):
    return pl.BlockSpec(shape, index_map, pipeline_mode=pl.Buffered(1))


def _layer_norm(z, g, b):
    mu = jnp.mean(z, axis=-1, keepdims=True)
    zc = z - mu
    var = jnp.mean(zc * zc, axis=-1, keepdims=True)
    return zc * lax.rsqrt(var + LN_EPS) * g + b


def _rms_norm(x, g, eps):
    return x * lax.rsqrt(jnp.mean(x * x, axis=-1, keepdims=True) + eps) * g


def _dot(a, b):
    return jnp.dot(a, b, preferred_element_type=F32)


def _dot_nt(a, b):
    return lax.dot_general(a, b, (((1,), (1,)), ((), ())), preferred_element_type=F32)


def _dot_tn(a, b):
    return lax.dot_general(a, b, (((0,), (0,)), ((), ())), preferred_element_type=F32)


def _mod_kernel(c_ref, w_ref, b_ref, o_ref):
    x = c_ref[...]
    s = (x * jax.nn.sigmoid(x)).astype(BF16)
    o_ref[0] = _dot(s, w_ref[0].astype(BF16)) + b_ref[0]


def _modulation(c_all, ada_w, ada_b):
    depth = ada_w.shape[0]
    return pl.pallas_call(
        _mod_kernel,
        grid=(depth, N_MOD),
        in_specs=[pl.BlockSpec((8, D_MODEL), lambda l, j: (0, 0)),
                  pl.BlockSpec((1, D_MODEL, D_MODEL), lambda l, j: (l, 0, j)),
                  pl.BlockSpec((1, 1, D_MODEL), lambda l, j: (l, 0, j))],
        out_specs=pl.BlockSpec((1, 8, D_MODEL), lambda l, j: (l, 0, j)),
        out_shape=jax.ShapeDtypeStruct((depth, 8, N_MOD * D_MODEL), F32),
        compiler_params=_cparams(2),
        name="modulation",
    )(c_all, ada_w, ada_b.reshape(depth, 1, N_MOD * D_MODEL))


def _ffn_kernel(*refs, n_first):
    h_refs, (mod_ref, wg_ref, wu_ref, wd_ref, g_ref, b_ref, o_ref) = refs[:-7], refs[-7:]
    shift, scale, gate = mod_ref[0, 0:1, :], mod_ref[0, 1:2, :], mod_ref[0, 2:3, :]
    for part in range(FFN_ROWS // TM):
        rows = slice(TM * part, TM * (part + 1))
        h = h_refs[0][rows, :]
        if len(h_refs) == 2:
            h = jnp.where(pl.program_id(0) < n_first, h, h_refs[1][rows, :])
        hm = (h * (1.0 + scale) + shift).astype(BF16)
        y = jnp.zeros((TM, D_MODEL), F32)
        for c in range(len(FF_SPLITS) - 1):
            cols = slice(FF_SPLITS[c], FF_SPLITS[c + 1])
            gt = _dot(hm, wg_ref[:, cols])
            up = _dot(hm, wu_ref[:, cols])
            act = (gt * jax.nn.sigmoid(gt) * up).astype(BF16)
            y = y + _dot(act, wd_ref[cols, :])
        o_ref[rows, :] = _layer_norm(DN_ALPHA * h + (0.5 * gate) * y, g_ref[...], b_ref[...])


def _ffn(h, mod3, wg, wu, wd, g, b, n_rows, seq_len, bsz, h_tail=None):
    n_first = h.shape[0] // FFN_ROWS
    mod_index = lambda i: jnp.minimum(i // (seq_len // FFN_ROWS), bsz)
    if h_tail is None:
        sources, src_specs = [h], [pl.BlockSpec((FFN_ROWS, D_MODEL), lambda i: (i, 0))]
    else:
        sources = [h, h_tail]
        src_specs = [pl.BlockSpec((FFN_ROWS, D_MODEL), lambda i: (jnp.minimum(i, n_first - 1), 0)),
                     pl.BlockSpec((FFN_ROWS, D_MODEL), lambda i: (jnp.maximum(i - n_first, 0), 0))]
    return pl.pallas_call(
        functools.partial(_ffn_kernel, n_first=n_first),
        grid=(n_rows // FFN_ROWS,),
        in_specs=src_specs + [
                  pl.BlockSpec((1, 3, D_MODEL), lambda i: (mod_index(i), 0, 0)),
                  _resident((D_MODEL, D_FF), lambda i: (0, 0)),
                  _resident((D_MODEL, D_FF), lambda i: (0, 0)),
                  _resident((D_FF, D_MODEL), lambda i: (0, 0)),
                  pl.BlockSpec((1, D_MODEL), lambda i: (0, 0)),
                  pl.BlockSpec((1, D_MODEL), lambda i: (0, 0))],
        out_specs=pl.BlockSpec((FFN_ROWS, D_MODEL), lambda i: (i, 0)),
        out_shape=jax.ShapeDtypeStruct((n_rows, D_MODEL), F32),
        compiler_params=_cparams(1),
        name="ffn_postnorm",
    )(*sources, mod3, wg, wu, wd, g, b)


def _ones_rows(n):
    row = lax.broadcasted_iota(jnp.int32, (BF16_ROWS, n), 0)
    return jnp.where(row == 0, 1.0, 0.0).astype(BF16)


def _rope(x, cos, sin_signed, quarter):
    lane = lax.broadcasted_iota(jnp.int32, x.shape, 1)
    first = (lane % (2 * quarter)) < quarter
    rot = jnp.where(first, pltpu.roll(x, LANES - quarter, 1), pltpu.roll(x, quarter, 1))
    return x * cos + rot * sin_signed


EV_QLAT = 0
EV_BQ = 256
EV_KVLAT = 768
EV_KPE = 896
EV_BK = 1024
EV_BV = 1536
EV_PCOLS = 2048


def _even_proj_kernel(h_ref, mod_ref, win_ref, gq_ref, gkv_ref, wuq_ref, wabs_ref,
                      cosa_ref, sina_ref, cosb_ref, sinb_ref,
                      qa_ref, ka_ref, ct_ref, qb_ref, kb_ref, vbt_ref):
    h = h_ref[...]
    hm = (h * (1.0 + mod_ref[0, 1:2, :]) + mod_ref[0, 0:1, :]).astype(BF16)
    p = _dot(hm, win_ref[...])
    cosa, sina, cosb, sinb = cosa_ref[...], sina_ref[...], cosb_ref[...], sinb_ref[...]

    qn = _rms_norm(p[:, EV_QLAT:EV_QLAT + A_Q_RANK], gq_ref[...], RMS_EPS).astype(BF16)
    q = _dot(qn, wuq_ref[...])
    n_nope = A_HEADS * A_NOPE
    qabs = _dot(q[:, :n_nope].astype(BF16), wabs_ref[...])
    qpe = [_rope(q[:, n_nope + LANES * s:n_nope + LANES * (s + 1)], cosa, sina, A_ROPE // 4) for s in range(2)]
    qpe_t = jnp.concatenate(qpe, axis=1).T
    for hd in range(A_HEADS):
        qa_ref[hd, 0:A_KV_RANK, :] = (qabs[:, A_KV_RANK * hd:A_KV_RANK * (hd + 1)].T * A_QSCALE).astype(BF16)
        qa_ref[hd, A_KV_RANK:A_QK, :] = (qpe_t[A_ROPE * hd:A_ROPE * (hd + 1), :] * A_QSCALE).astype(BF16)

    cn = _rms_norm(p[:, EV_KVLAT:EV_KVLAT + A_KV_RANK], gkv_ref[...], RMS_EPS)
    kpe = _rope(p[:, EV_KPE:EV_KPE + LANES], cosa, sina, A_ROPE // 4)
    ka_ref[:, 0:A_KV_RANK] = cn.astype(BF16)
    ka_ref[:, A_KV_RANK:A_QK] = kpe[:, 0:A_ROPE].astype(BF16)
    ones_rows = _ones_rows(TM)
    cnt = jnp.concatenate([cn.T.astype(BF16), ones_rows], axis=0)
    for s in range(TM // VB):
        ct_ref[s] = cnt[:, VB * s:VB * (s + 1)]

    for hd in range(B_HEADS):
        sl = slice(LANES * hd, LANES * (hd + 1))
        qb_ref[hd] = (_rope(p[:, EV_BQ:EV_BQ + B_QK][:, sl], cosb, sinb, B_HD // 4).T * B_QSCALE).astype(BF16)
        kb_ref[hd] = _rope(p[:, EV_BK:EV_BK + B_QK][:, sl], cosb, sinb, B_HD // 4).astype(BF16)
        vt = jnp.concatenate([p[:, EV_BV:EV_BV + B_QK][:, sl].T.astype(BF16), ones_rows], axis=0)
        for s in range(TM // VB):
            vbt_ref[hd, s] = vt[:, VB * s:VB * (s + 1)]


def _even_proj(h, mod3, win, gq, gkv, wuq, wabs, tabs, n_tiles, mod_index, tab_index):
    t_all = n_tiles * TM
    row = lambda i: (i, 0)
    tab = lambda i: (tab_index(i), 0)
    const = lambda i: (0, 0)
    return pl.pallas_call(
        _even_proj_kernel,
        grid=(n_tiles,),
        in_specs=[pl.BlockSpec((TM, D_MODEL), row),
                  pl.BlockSpec((1, 3, D_MODEL), lambda i: (mod_index(i), 0, 0)),
                  _resident((D_MODEL, EV_PCOLS), const),
                  pl.BlockSpec((1, A_Q_RANK), const),
                  pl.BlockSpec((1, A_KV_RANK), const),
                  _resident((A_Q_RANK, A_HEADS * (A_NOPE + A_ROPE)), const),
                  _resident((A_HEADS * A_NOPE, A_HEADS * A_KV_RANK), const),
                  pl.BlockSpec((TM, LANES), tab), pl.BlockSpec((TM, LANES), tab),
                  pl.BlockSpec((TM, LANES), tab), pl.BlockSpec((TM, LANES), tab)],
        out_specs=[pl.BlockSpec((A_HEADS, A_QK, TM), lambda i: (0, 0, i)),
                   pl.BlockSpec((TM, A_QK), row),
                   pl.BlockSpec((TM // VB, DV, VB), lambda i: (i, 0, 0)),
                   pl.BlockSpec((B_HEADS, LANES, TM), lambda i: (0, 0, i)),
                   pl.BlockSpec((B_HEADS, TM, LANES), lambda i: (0, i, 0)),
                   pl.BlockSpec((B_HEADS, TM // VB, DV, VB), lambda i: (0, i, 0, 0))],
        out_shape=[jax.ShapeDtypeStruct((A_HEADS, A_QK, t_all), BF16),
                   jax.ShapeDtypeStruct((t_all, A_QK), BF16),
                   jax.ShapeDtypeStruct((t_all // VB, DV, VB), BF16),
                   jax.ShapeDtypeStruct((B_HEADS, LANES, t_all), BF16),
                   jax.ShapeDtypeStruct((B_HEADS, t_all, LANES), BF16),
                   jax.ShapeDtypeStruct((B_HEADS, t_all // VB, DV, VB), BF16)],
        compiler_params=_cparams(1),
        name="even_proj",
    )(h, mod3, win, gq, gkv, wuq, wabs, *tabs)


ITEMS = 1
N_SLOTS = 8
KEY_GROUP = 2


class _Flash:
    def __init__(self, s_refs, p_refs, acc_ref):
        self.s, self.p, self.acc = s_refs, p_refs, acc_ref

    def qk(self, b, k, q_t):
        rows = k.shape[0]
        s = _dot(k, q_t)
        self.s[b % N_SLOTS][0:rows, :] = s
        return jnp.max(s.reshape(rows // SUBLANES, SUBLANES, NQ), axis=0)

    def sm(self, blocks, rows, maxima, m_old):
        m8 = maxima[0]
        for x in maxima[1:]:
            m8 = jnp.maximum(m8, x)
        m_new = jnp.maximum(m_old, jnp.max(m8, axis=0, keepdims=True))
        alpha = jnp.exp2(m_old - m_new)
        for b in blocks:
            s_ref, p_ref = self.s[b % N_SLOTS], self.p[b % N_SLOTS]
            for r in range(rows // STRIP):
                p_ref[STRIP * r:STRIP * (r + 1), :] = jnp.exp2(s_ref[STRIP * r:STRIP * (r + 1), :] - m_new).astype(BF16)
        return m_new, alpha

    def pv(self, blocks, vt_lists, alpha):
        pv = None
        for b, vts in zip(blocks, vt_lists):
            for i, vt in enumerate(vts):
                d = _dot(vt, self.p[b % N_SLOTS][VB * i:VB * (i + 1), :])
                pv = d if pv is None else pv + d
        self.acc[...] = pv if alpha is None else alpha * self.acc[...] + pv

    def run(self, q_t, key_blocks, group):
        groups, i = [], 0
        while i < len(key_blocks):
            j = i + 1
            while j < len(key_blocks) and j - i < group and key_blocks[j][0].shape == key_blocks[i][0].shape:
                j += 1
            groups.append(list(range(i, j)))
            i = j
        m = jnp.full((1, NQ), NEG_INF, F32)
        maxima, alphas = {}, {}
        for h in range(len(groups) + 2):
            if h < len(groups):
                for b in groups[h]:
                    maxima[b] = self.qk(b, key_blocks[b][0], q_t)
            if 0 <= h - 1 < len(groups):
                g = groups[h - 1]
                m, alphas[h - 1] = self.sm(g, key_blocks[g[0]][0].shape[0], [maxima[b] for b in g], m)
            if 0 <= h - 2 < len(groups):
                g = groups[h - 2]
                self.pv(g, [key_blocks[b][1] for b in g], alphas[h - 2] if h > 2 else None)

    def normalised(self):
        return self.acc[0:LANES, :] * (1.0 / self.acc[LANES:LANES + 1, :])


def _flash_scratch():
    per_item = ([pltpu.VMEM((TK, NQ), F32)] * N_SLOTS + [pltpu.VMEM((TK, NQ), BF16)] * N_SLOTS
                + [pltpu.VMEM((DV, NQ), F32)])
    return per_item * ITEMS


def _flash_from_scratch(refs):
    per = 2 * N_SLOTS + 1
    return [_Flash(r[0:N_SLOTS], r[N_SLOTS:2 * N_SLOTS], r[2 * N_SLOTS])
            for r in (refs[per * u:per * (u + 1)] for u in range(ITEMS))]


def _block_diag_queries_t(q_t):
    row = lax.broadcasted_iota(jnp.int32, q_t.shape, 0)
    zero = jnp.zeros_like(q_t)
    half = LANES // 2
    return jnp.concatenate([jnp.where(row < half, q_t, zero), jnp.where(row >= half, q_t, zero)], axis=1)


def _block_diag_queries(q12):
    lane = lax.broadcasted_iota(jnp.int32, q12.shape, 1)
    zero = jnp.zeros_like(q12)
    half = LANES // 2
    return jnp.concatenate([jnp.where(lane < half, q12, zero), jnp.where(lane >= half, q12, zero)], axis=0)


def _mla_kernel(qa_ref, kl_ref, kc_ref, ctl_ref, ctc_ref, wuvt_ref, o_ref, *scratch, n_lat_steps, n_qb_lat):
    fls = _flash_from_scratch(scratch)
    is_latent = pl.program_id(1) < n_qb_lat
    ctx_block = lambda: (kc_ref[...], [ctc_ref[0]])
    lat_block = lambda t: (kl_ref[TK * t:TK * (t + 1), :], [ctl_ref[(TK // VB) * t + i] for i in range(TK // VB)])

    def group(gg, carry):
        gs = [ITEMS * gg + u for u in range(ITEMS)]
        q_ts = [jnp.concatenate([qa_ref[2 * g], qa_ref[2 * g + 1]], axis=1) for g in gs]

        @pl.when(is_latent)
        def _():
            for fl, q_t in zip(fls, q_ts):
                fl.run(q_t, [lat_block(t) for t in range(n_lat_steps)] + [ctx_block()], KEY_GROUP)

        @pl.when(jnp.logical_not(is_latent))
        def _():
            for fl, q_t in zip(fls, q_ts):
                fl.run(q_t, [ctx_block()], KEY_GROUP)

        for fl, g in zip(fls, gs):
            o_lat = fl.normalised().astype(BF16)
            for i in range(2):
                o = _dot(wuvt_ref[2 * g + i], o_lat[:, TQ * i:TQ * (i + 1)])
                o_ref[pl.ds(pl.multiple_of((2 * g + i) * A_V, A_V), A_V), :] = o.astype(BF16)
        return carry

    lax.fori_loop(0, A_HEADS // 2 // ITEMS, group, 0)


def _mla(qa, ka, ct, wuvt, bsz, n, m):
    t_lat = bsz * n
    t_all = t_lat + bsz * m
    nqb = n // TQ
    qblk = lambda b, j: jnp.where(j < nqb, b * nqb + j, t_lat // TQ + b)
    kern = functools.partial(_mla_kernel, n_lat_steps=n // TK, n_qb_lat=nqb)
    return pl.pallas_call(
        kern,
        grid=(bsz, nqb + 1),
        in_specs=[pl.BlockSpec((A_HEADS, A_QK, TQ), lambda b, j: (0, 0, qblk(b, j))),
                  _resident((n, A_QK), lambda b, j: (b, 0)),
                  pl.BlockSpec((m, A_QK), lambda b, j: (t_lat // m + b, 0)),
                  _resident((n // VB, DV, VB), lambda b, j: (b, 0, 0)),
                  pl.BlockSpec((m // VB, DV, VB), lambda b, j: (t_lat // m + b, 0, 0)),
                  pl.BlockSpec((A_HEADS, A_V, A_KV_RANK), lambda b, j: (0, 0, 0))],
        out_specs=pl.BlockSpec((A_HEADS * A_V, TQ), lambda b, j: (0, qblk(b, j))),
        out_shape=jax.ShapeDtypeStruct((A_HEADS * A_V, t_all), BF16),
        scratch_shapes=_flash_scratch(),
        compiler_params=_cparams(2),
        name="mla_attention",
    )(qa, ka, ka, ct, ct, wuvt)


def _diff_kernel(qb_ref, kl_ref, kc_ref, vl_ref, vc_ref, lam_ref, gsub_ref, o_ref, *scratch, n_lat_steps, n_qb_lat,
                 lam_init):
    fls = _flash_from_scratch(scratch)
    is_latent = pl.program_id(1) < n_qb_lat
    lv = lam_ref[...]
    lam = (jnp.exp(jnp.sum(lv[0:1] * lv[1:2], axis=1, keepdims=True))
           - jnp.exp(jnp.sum(lv[2:3] * lv[3:4], axis=1, keepdims=True)) + lam_init)

    def group(gg, carry):
        hds = [ITEMS * gg + u for u in range(ITEMS)]
        q_ts = [_block_diag_queries_t(qb_ref[hd]) for hd in hds]
        ctx_block = lambda hd: (kc_ref[hd], [vc_ref[hd, 0]])
        lat_block = lambda hd, t: (kl_ref[hd, TK * t:TK * (t + 1), :],
                                   [vl_ref[hd, (TK // VB) * t + i] for i in range(TK // VB)])

        @pl.when(is_latent)
        def _():
            for fl, q_t, hd in zip(fls, q_ts, hds):
                fl.run(q_t, [lat_block(hd, t) for t in range(n_lat_steps)] + [ctx_block(hd)], KEY_GROUP)

        @pl.when(jnp.logical_not(is_latent))
        def _():
            for fl, q_t, hd in zip(fls, q_ts, hds):
                fl.run(q_t, [ctx_block(hd)], KEY_GROUP)

        for fl, hd in zip(fls, hds):
            on = fl.normalised()
            o = on[:, 0:TQ] - lam * on[:, TQ:2 * TQ]
            o = o * lax.rsqrt(jnp.mean(o * o, axis=0, keepdims=True) + SUBLN_EPS) * gsub_ref[...]
            o_ref[pl.ds(pl.multiple_of(hd * LANES, LANES), LANES), :] = (o * (1.0 - lam_init)).astype(BF16)
        return carry

    lax.fori_loop(0, B_HEADS // ITEMS, group, 0)


def _diff(qb, kb, vbt, lam_vec, gsub_col, lam_init, bsz, n, m):
    t_lat = bsz * n
    t_all = t_lat + bsz * m
    nqb = n // TQ
    qblk = lambda b, j: jnp.where(j < nqb, b * nqb + j, t_lat // TQ + b)
    kern = functools.partial(_diff_kernel, n_lat_steps=n // TK, n_qb_lat=nqb, lam_init=lam_init)
    return pl.pallas_call(
        kern,
        grid=(bsz, nqb + 1),
        in_specs=[pl.BlockSpec((B_HEADS, LANES, TQ), lambda b, j: (0, 0, qblk(b, j))),
                  _resident((B_HEADS, n, LANES), lambda b, j: (0, b, 0)),
                  pl.BlockSpec((B_HEADS, m, LANES), lambda b, j: (0, t_lat // m + b, 0)),
                  _resident((B_HEADS, n // VB, DV, VB), lambda b, j: (0, b, 0, 0)),
                  pl.BlockSpec((B_HEADS, m // VB, DV, VB), lambda b, j: (0, t_lat // m + b, 0, 0)),
                  pl.BlockSpec((4, B_HD), lambda b, j: (0, 0)),
                  pl.BlockSpec((2 * B_HD, 1), lambda b, j: (0, 0))],
        out_specs=pl.BlockSpec((B_HEADS * LANES, TQ), lambda b, j: (0, qblk(b, j))),
        out_shape=jax.ShapeDtypeStruct((B_HEADS * LANES, t_all), BF16),
        scratch_shapes=_flash_scratch(),
        compiler_params=_cparams(2),
        name="diff_attention",
    )(qb, kb, kb, vbt, vbt, lam_vec, gsub_col)


def _outproj_kernel(h_ref, mod_ref, a_ref, b_ref, w_ref, g_ref, beta_ref, o_ref, *, a_transposed):
    half = w_ref.shape[0] // 2
    ya = _dot_tn(a_ref[...], w_ref[0:half, :]) if a_transposed else _dot(a_ref[...], w_ref[0:half, :])
    y = ya + _dot_tn(b_ref[...], w_ref[half:2 * half, :])
    z = DN_ALPHA * h_ref[...] + mod_ref[0, 2:3, :] * y
    o_ref[...] = _layer_norm(z, g_ref[...], beta_ref[...])


def _outproj(h, mod3, mix_a, mix_b, w, g, b, n_tiles, mod_index, a_transposed):
    half = w.shape[0] // 2
    a_spec = (pl.BlockSpec((half, TM), lambda i: (0, i)) if a_transposed
              else pl.BlockSpec((TM, half), lambda i: (i, 0)))
    return pl.pallas_call(
        functools.partial(_outproj_kernel, a_transposed=a_transposed),
        grid=(n_tiles,),
        in_specs=[pl.BlockSpec((TM, D_MODEL), lambda i: (i, 0)),
                  pl.BlockSpec((1, 3, D_MODEL), lambda i: (mod_index(i), 0, 0)),
                  a_spec,
                  pl.BlockSpec((half, TM), lambda i: (0, i)),
                  _resident((2 * half, D_MODEL), lambda i: (0, 0)),
                  pl.BlockSpec((1, D_MODEL), lambda i: (0, 0)),
                  pl.BlockSpec((1, D_MODEL), lambda i: (0, 0))],
        out_specs=pl.BlockSpec((TM, D_MODEL), lambda i: (i, 0)),
        out_shape=jax.ShapeDtypeStruct((n_tiles * TM, D_MODEL), F32),
        compiler_params=_cparams(1),
        name="outproj_postnorm",
    )(h, mod3, mix_a, mix_b, w, g, b)


def _odd_proj_kernel(h_ref, mod_ref, win_ref, u_ref, q_ref, k_ref, vt_ref):
    h = h_ref[...]
    hm = (h * (1.0 + mod_ref[0, 1:2, :]) + mod_ref[0, 0:1, :]).astype(BF16)
    p = _dot(hm, win_ref[...])
    u_ref[...] = p[:, 0:C_WIDTH]
    for s in range(D_WIDTH // LANES):
        sl = slice(LANES * s, LANES * (s + 1))
        q_ref[s] = (p[:, C_WIDTH:C_WIDTH + D_WIDTH][:, sl] * D_QSCALE).astype(BF16)
        k_ref[s] = p[:, C_WIDTH + D_WIDTH:C_WIDTH + 2 * D_WIDTH][:, sl].astype(BF16)
        vt = p[:, C_WIDTH + 2 * D_WIDTH:C_WIDTH + 3 * D_WIDTH][:, sl].T.astype(BF16)
        for i in range(TM // VB):
            vt_ref[s, i] = vt[:, VB * i:VB * (i + 1)]


def _odd_proj(h, mod3, win, n_tiles, mod_index):
    t_all = n_tiles * TM
    slabs = D_WIDTH // LANES
    return pl.pallas_call(
        _odd_proj_kernel,
        grid=(n_tiles,),
        in_specs=[pl.BlockSpec((TM, D_MODEL), lambda i: (i, 0)),
                  pl.BlockSpec((1, 3, D_MODEL), lambda i: (mod_index(i), 0, 0)),
                  _resident((D_MODEL, C_WIDTH + 3 * D_WIDTH), lambda i: (0, 0))],
        out_specs=[pl.BlockSpec((TM, C_WIDTH), lambda i: (i, 0)),
                   pl.BlockSpec((slabs, TM, LANES), lambda i: (0, i, 0)),
                   pl.BlockSpec((slabs, TM, LANES), lambda i: (0, i, 0)),
                   pl.BlockSpec((slabs, TM // VB, LANES, VB), lambda i: (0, i, 0, 0))],
        out_shape=[jax.ShapeDtypeStruct((t_all, C_WIDTH), F32),
                   jax.ShapeDtypeStruct((slabs, t_all, LANES), BF16),
                   jax.ShapeDtypeStruct((slabs, t_all, LANES), BF16),
                   jax.ShapeDtypeStruct((slabs, t_all // VB, LANES, VB), BF16)],
        compiler_params=_cparams(1),
        name="odd_proj",
    )(h, mod3, win)


def _pool_kernel(prev_ref, u_ref, next_ref, w_ref, ps_ref, o_ref, ext_ref, *, tiles_per_seq):
    t = pl.program_id(0) % tiles_per_seq
    n = tiles_per_seq * TM
    ext_ref[0:POOL_HALO, :] = jnp.where(t > 0, prev_ref[...], 0.0)
    ext_ref[POOL_HALO:POOL_HALO + TM, :] = u_ref[...]
    ext_ref[POOL_HALO + TM:2 * POOL_HALO + TM, :] = jnp.where(t < tiles_per_seq - 1, next_ref[...], 0.0)
    pos = t * TM + lax.broadcasted_iota(jnp.int32, (TM, 1), 0)
    for g, w in enumerate(C_WINDOWS):
        left = w // 2
        right = w - 1 - left
        lanes = slice(C_GW * g, C_GW * (g + 1))
        tot = ext_ref[POOL_HALO - left:POOL_HALO - left + TM, lanes]
        for k in range(1 - left, right + 1):
            tot = tot + ext_ref[POOL_HALO + k:POOL_HALO + k + TM, lanes]
        cnt = jnp.minimum(pos + right + 1, n) - jnp.maximum(pos - left, 0)
        pooled = tot / cnt.astype(F32) - u_ref[:, lanes]
        y = _dot(pooled.astype(BF16), w_ref[g])
        o_ref[:, lanes] = (y * ps_ref[:, lanes]).astype(BF16)


def _pool(u, w_pool, pool_scale, n_tiles, tiles_per_seq):
    per = TM // POOL_HALO
    last = u.shape[0] // POOL_HALO - 1
    return pl.pallas_call(
        functools.partial(_pool_kernel, tiles_per_seq=tiles_per_seq),
        grid=(n_tiles,),
        in_specs=[pl.BlockSpec((POOL_HALO, C_WIDTH), lambda i: (jnp.maximum(i * per - 1, 0), 0)),
                  pl.BlockSpec((TM, C_WIDTH), lambda i: (i, 0)),
                  pl.BlockSpec((POOL_HALO, C_WIDTH), lambda i: (jnp.minimum((i + 1) * per, last), 0)),
                  pl.BlockSpec((C_GROUPS, C_GW, C_GW), lambda i: (0, 0, 0)),
                  pl.BlockSpec((1, C_WIDTH), lambda i: (0, 0))],
        out_specs=pl.BlockSpec((TM, C_WIDTH), lambda i: (i, 0)),
        out_shape=jax.ShapeDtypeStruct((n_tiles * TM, C_WIDTH), BF16),
        scratch_shapes=[pltpu.VMEM((TM + 2 * POOL_HALO, C_WIDTH), F32)],
        compiler_params=_cparams(1),
        name="multiscale_pool",
    )(u, u, u, w_pool, pool_scale)


def _na_row_table(rows):
    qr_per = TQ // GRID_W
    nblk = rows // qr_per
    g_of = (0, 1, nblk - 1)
    gb_of = tuple(min(max(g - 1, 0), nblk - 3) for g in g_of)
    table = {}
    for p in range(NA_PATTERNS):
        for kr in range(NA_KEY_ROWS):
            for qr in range(qr_per):
                r = qr_per * g_of[p] + qr
                ka = qr_per * gb_of[p] + kr
                r0 = min(max(r - NA_ROWS // 2, 0), rows - NA_ROWS)
                table[p, kr, qr] = (r0 <= ka < r0 + NA_ROWS, ka - r + NA_ROWS - 1)
    return table


def _na_bias_kernel(rpb_ref, o_ref, *, rows):
    hd = pl.program_id(0)
    n_r, n_c = 2 * NA_ROWS - 1, 2 * NA_COLS - 1
    qr_per = TQ // GRID_W
    kc = lax.broadcasted_iota(jnp.int32, (GRID_W, TQ), 0)
    q = lax.broadcasted_iota(jnp.int32, (GRID_W, TQ), 1)
    qc, qrow = q % GRID_W, q // GRID_W
    idx = kc - qc + NA_COLS - 1
    cs = jnp.clip(qc - NA_COLS // 2, 0, GRID_W - NA_COLS)
    col_ok = (kc >= cs) & (kc < cs + NA_COLS)
    neg = jnp.full((GRID_W, TQ), NEG_INF, F32)
    toeplitz = []
    for dr in range(n_r):
        acc = neg
        for j in range(n_c):
            acc = jnp.where(idx == j, rpb_ref[hd * n_r * n_c + dr * n_c + j], acc)
        toeplitz.append(jnp.where(col_ok, acc * LOG2E, neg))
    table = _na_row_table(rows)
    for p in range(NA_PATTERNS):
        for kr in range(NA_KEY_ROWS):
            blk = neg
            for qr in range(qr_per):
                ok, ridx = table[p, kr, qr]
                if ok:
                    blk = jnp.where(qrow == qr, toeplitz[ridx], blk)
            o_ref[p, 0, GRID_W * kr:GRID_W * (kr + 1), :] = blk


def _na_bias(rpb, rows):
    n_r, n_c = 2 * NA_ROWS - 1, 2 * NA_COLS - 1
    n_keys = NA_KEY_ROWS * GRID_W
    return pl.pallas_call(
        functools.partial(_na_bias_kernel, rows=rows),
        grid=(D_HEADS,),
        in_specs=[pl.BlockSpec(memory_space=pltpu.SMEM)],
        out_specs=pl.BlockSpec((NA_PATTERNS, 1, n_keys, TQ), lambda i: (0, i // 2, 0, i % 2)),
        out_shape=jax.ShapeDtypeStruct((NA_PATTERNS, D_HEADS // 2, n_keys, 2 * TQ), F32),
        compiler_params=_cparams(1),
        name="na_bias",
    )(rpb.reshape(D_HEADS * n_r * n_c))


def _na_kernel(q_ref, kl_ref, kc_ref, vl_ref, vc_ref, bias_ref, o_ref, *, nqb):
    n_nb = NA_KEY_ROWS * GRID_W
    half = LANES // 2
    for i in range(NA_GS):
        g = pl.program_id(2) * NA_GS + i
        gb = jnp.clip(g - 1, 0, nqb - 3)
        pat = jnp.where(g == 0, 0, jnp.where(g == nqb - 1, 2, 1))
        cols = slice(TQ * i, TQ * (i + 1))
        qbd = _block_diag_queries(q_ref[0, cols, :])
        k_nb = kl_ref[0, pl.ds(pl.multiple_of(gb * TQ, TQ), n_nb), :]
        s_nb = _dot_nt(k_nb, qbd) + bias_ref[pat, 0]
        s_cx = _dot_nt(kc_ref[0], qbd)
        mx = jnp.maximum(jnp.max(s_nb, axis=0, keepdims=True), jnp.max(s_cx, axis=0, keepdims=True))
        p_nb = jnp.exp2(s_nb - mx)
        p_cx = jnp.exp2(s_cx - mx)
        l = jnp.sum(p_nb, axis=0, keepdims=True) + jnp.sum(p_cx, axis=0, keepdims=True)
        pb = p_nb.astype(BF16)
        acc = _dot(vc_ref[0, 0], p_cx.astype(BF16))
        for j in range(n_nb // VB):
            acc = acc + _dot(vl_ref[0, gb + j], pb[VB * j:VB * (j + 1)])
        on = acc * (1.0 / l)
        o_ref[0:half, cols] = on[0:half, 0:TQ].astype(BF16)
        o_ref[half:LANES, cols] = on[half:LANES, TQ:2 * TQ].astype(BF16)


def _natten(qn, kn, vnt, bias, bsz, n, m):
    t_lat = bsz * n
    nqb = n // TQ
    steps = nqb // NA_GS
    slabs = D_WIDTH // LANES
    return pl.pallas_call(
        functools.partial(_na_kernel, nqb=nqb),
        grid=(bsz, slabs, steps),
        in_specs=[pl.BlockSpec((1, NA_GS * TQ, LANES), lambda b, s, g: (s, b * steps + g, 0)),
                  pl.BlockSpec((1, n, LANES), lambda b, s, g: (s, b, 0)),
                  pl.BlockSpec((1, m, LANES), lambda b, s, g: (s, t_lat // m + b, 0)),
                  pl.BlockSpec((1, n // VB, LANES, VB), lambda b, s, g: (s, b, 0, 0)),
                  pl.BlockSpec((1, m // VB, LANES, VB), lambda b, s, g: (s, t_lat // m + b, 0, 0)),
                  pl.BlockSpec((NA_PATTERNS, 1, NA_KEY_ROWS * GRID_W, 2 * TQ), lambda b, s, g: (0, s, 0, 0))],
        out_specs=pl.BlockSpec((LANES, NA_GS * TQ), lambda b, s, g: (s, b * steps + g)),
        out_shape=jax.ShapeDtypeStruct((D_WIDTH, t_lat), BF16),
        compiler_params=_cparams(3),
        name="neighbourhood_attention",
    )(qn, kn, kn, vnt, vnt, bias)


def _rope_tables(n, rot_dim):
    t = np.arange(n)
    row = (t // GRID_W).astype(np.float32)
    col = (t % GRID_W).astype(np.float32)
    quarter = rot_dim // 4
    inv = jnp.asarray(ROPE_BASE, F32) ** (-jnp.arange(quarter, dtype=F32) / quarter)
    ang_r = jnp.asarray(row)[:, None] * inv[None, :]
    ang_c = jnp.asarray(col)[:, None] * inv[None, :]
    ang = jnp.concatenate([ang_r, ang_r, ang_c, ang_c], -1)
    sign = np.where((np.arange(rot_dim) % (2 * quarter)) < quarter, -1.0, 1.0).astype(np.float32)
    cos = jnp.tile(jnp.cos(ang), (1, LANES // rot_dim))
    sin = jnp.tile(jnp.sin(ang) * sign[None, :], (1, LANES // rot_dim))
    cos = jnp.concatenate([cos, jnp.ones((TM, LANES), F32)], 0)
    sin = jnp.concatenate([sin, jnp.zeros((TM, LANES), F32)], 0)
    return cos, sin


def _even_weights(w_in, w_uq, w_ukv):
    o = A_Q_RANK + B_QK
    kpe = jnp.pad(w_in[:, o + A_KV_RANK:o + A_KV_RANK + A_ROPE], ((0, 0), (0, LANES - A_ROPE)))
    win = jnp.concatenate([w_in[:, 0:A_Q_RANK], w_in[:, A_Q_RANK:o], w_in[:, o:o + A_KV_RANK], kpe,
                           w_in[:, o + A_KV_RANK + A_ROPE:o + A_KV_RANK + A_ROPE + B_QK],
                           w_in[:, o + A_KV_RANK + A_ROPE + B_QK:]], axis=1).astype(BF16)
    uq = w_uq.reshape(A_Q_RANK, A_HEADS, A_NOPE + A_ROPE)
    wuq = jnp.concatenate([uq[:, :, :A_NOPE].reshape(A_Q_RANK, A_HEADS * A_NOPE),
                           uq[:, :, A_NOPE:].reshape(A_Q_RANK, A_HEADS * A_ROPE)], axis=1).astype(BF16)
    ukv = w_ukv.reshape(A_KV_RANK, A_HEADS, A_NOPE + A_V)
    uk_t = jnp.transpose(ukv[:, :, :A_NOPE], (1, 2, 0))
    wabs = jnp.zeros((A_HEADS, A_NOPE, A_HEADS, A_KV_RANK), F32)
    wabs = wabs.at[np.arange(A_HEADS), :, np.arange(A_HEADS), :].set(uk_t)
    wabs = wabs.reshape(A_HEADS * A_NOPE, A_HEADS * A_KV_RANK).astype(BF16)
    wuvt = jnp.transpose(ukv[:, :, A_NOPE:], (1, 2, 0)).astype(BF16)
    return win, wuq, wabs, wuvt


def kernel(x, c, ctx, c_ctx, ada_w, ada_b, ln_g, ln_b, ffn_w_gate, ffn_w_up, ffn_w_down, ev_w_in, ev_w_out,
           ev_g_qlat, ev_g_kvlat, ev_w_uq, ev_w_ukv, ev_lam, ev_g_sub, od_w_in, od_w_out, od_w_pool,
           od_pool_scale, od_rpb):
    bsz, n, d = x.shape
    m = ctx.shape[1]
    assert d == D_MODEL and ada_w.shape[0] == DEPTH == 2
    assert m == TQ and n % FFN_ROWS == 0 and n % GRID_W == 0 and (bsz * m) % FFN_ROWS == 0 and bsz + 1 <= 8
    assert n // TQ >= 3 and (n // TQ) % NA_GS == 0
    t_lat = bsz * n
    t_all = t_lat + bsz * m
    tiles_lat, tiles_all, tiles_seq = t_lat // TM, t_all // TM, n // TM
    mod_index = lambda i: jnp.minimum(i // tiles_seq, bsz)
    tab_index = lambda i: jnp.where(i < tiles_lat, i % tiles_seq, tiles_seq)

    c_all = jnp.zeros((8, D_MODEL), F32).at[:bsz].set(c).at[bsz].set(c_ctx)
    mod = _modulation(c_all, ada_w, ada_b).reshape(DEPTH, 8, N_MOD, D_MODEL)
    row = lambda v: v.reshape(1, -1)
    wg, wu, wd = ffn_w_gate.astype(BF16), ffn_w_up.astype(BF16), ffn_w_down.astype(BF16)

    h = _ffn(x.reshape(t_lat, D_MODEL), mod[0, :, 0:3], wg[0, 0], wu[0, 0], wd[0, 0], row(ln_g[0, 0]), row(ln_b[0, 0]),
             t_all, n, bsz, h_tail=ctx.reshape(bsz * m, D_MODEL))
    win, wuq, wabs, wuvt = _even_weights(ev_w_in[0], ev_w_uq[0], ev_w_ukv[0])
    tabs = _rope_tables(n, A_ROPE) + _rope_tables(n, B_HD)
    qa, ka, ct, qb, kb, vbt = _even_proj(h, mod[0, :, 3:6], win, row(ev_g_qlat[0]), row(ev_g_kvlat[0]), wuq, wabs,
                                         tabs, tiles_all, mod_index, tab_index)
    lam_init = 0.8 - 0.6 * math.exp(-0.3 * 0)
    mix_a = _mla(qa, ka, ct, wuvt, bsz, n, m)
    mix_b = _diff(qb, kb, vbt, ev_lam[0], ev_g_sub[0].reshape(2 * B_HD, 1), lam_init, bsz, n, m)
    h = _outproj(h, mod[0, :, 3:6], mix_a, mix_b, ev_w_out[0].astype(BF16), row(ln_g[0, 1]), row(ln_b[0, 1]),
                 tiles_all, mod_index, True)
    h = _ffn(h, mod[0, :, 6:9], wg[0, 1], wu[0, 1], wd[0, 1], row(ln_g[0, 2]), row(ln_b[0, 2]), t_all, n, bsz)

    h = _ffn(h, mod[1, :, 0:3], wg[1, 0], wu[1, 0], wd[1, 0], row(ln_g[1, 0]), row(ln_b[1, 0]), t_all, n, bsz)
    u, qn, kn, vnt = _odd_proj(h, mod[1, :, 3:6], od_w_in[0].astype(BF16), tiles_all, mod_index)
    y_pool = _pool(u, od_w_pool[0].astype(BF16), row(od_pool_scale[0]), tiles_lat, tiles_seq)
    bias = _na_bias(od_rpb[0], n // GRID_W)
    y_na = _natten(qn, kn, vnt, bias, bsz, n, m)
    h = _outproj(h, mod[1, :, 3:6], y_pool, y_na, od_w_out[0].astype(BF16), row(ln_g[1, 1]), row(ln_b[1, 1]),
                 tiles_lat, mod_index, False)
    h = _ffn(h, mod[1, :, 6:9], wg[1, 1], wu[1, 1], wd[1, 1], row(ln_g[1, 2]), row(ln_b[1, 2]), t_lat, n, bsz)
    return h.reshape(bsz, n, D_MODEL)
```

```python
import functools
import math

import numpy as np
import jax
import jax.numpy as jnp
from jax import lax
from jax.experimental import pallas as pl
from jax.experimental.pallas import tpu as pltpu

F32 = jnp.float32
BF16 = jnp.bfloat16

D_MODEL = 1024
DEPTH = 2
GRID_W = 64
N_MOD = 9
DN_ALPHA = float((2 * DEPTH) ** 0.25)
LN_EPS = 1e-6
RMS_EPS = 1e-6
SUBLN_EPS = 1e-5
ROPE_BASE = 10000.0
NEG_INF = -1e30
D_FF = 2816

A_HEADS = 8
A_NOPE = 64
A_ROPE = 32
A_V = 64
A_Q_RANK = 256
A_KV_RANK = 128
A_SCALE = float((A_NOPE + A_ROPE) ** -0.5)
A_QK = A_KV_RANK + A_ROPE

B_HEADS = 4
B_HD = 64
B_SCALE = float(B_HD ** -0.5)
B_QK = B_HEADS * 2 * B_HD

C_GROUPS = 4
C_WINDOWS = (2, 4, 8, 16)
C_GW = 128
C_WIDTH = C_GROUPS * C_GW

D_HEADS = 8
D_HD = 64
D_SCALE = float(D_HD ** -0.5)
D_WIDTH = D_HEADS * D_HD
NA_ROWS = 8
NA_COLS = 16

LOG2E = math.log2(math.e)
A_QSCALE = A_SCALE * LOG2E
B_QSCALE = B_SCALE * LOG2E
D_QSCALE = D_SCALE * LOG2E

LANES = 128
SUBLANES = 8
TM = 512
FFN_ROWS = 2 * TM
TQ = 256
NQ = 2 * TQ
TK = 512
VB = 256
STRIP = 64
BF16_ROWS = 16
DV = LANES + BF16_ROWS
MXU_DIM = 256
FF_SPLITS = (0, 6 * MXU_DIM, D_FF)
POOL_HALO = 8
NA_KEY_ROWS = 12
NA_PATTERNS = 3
NA_GS = 8
VMEM_LIMIT = 56 * 1024 * 1024


def _cparams(n_axes):
    return pltpu.CompilerParams(dimension_semantics=("arbitrary",) * n_axes,
                                vmem_limit_bytes=VMEM_LIMIT)


def _resident(shape, index_map):
    return pl.BlockSpec(shape, index_map, pipeline_mode=pl.Buffered(1))


def _layer_norm(z, g, b):
    mu = jnp.mean(z, axis=-1, keepdims=True)
    zc = z - mu
    var = jnp.mean(zc * zc, axis=-1, keepdims=True)
    return zc * lax.rsqrt(var + LN_EPS) * g + b


def _rms_norm(x, g, eps):
    return x * lax.rsqrt(jnp.mean(x * x, axis=-1, keepdims=True) + eps) * g


def _dot(a, b):
    return jnp.dot(a, b, preferred_element_type=F32)


def _dot_tn(a, b):
    return lax.dot_general(a, b, (((0,), (0,)), ((), ())), preferred_element_type=F32)


def _mod_kernel(c_ref, w_ref, b_ref, o_ref):
    x = c_ref[...]
    s = (x * jax.nn.sigmoid(x)).astype(BF16)
    o_ref[0] = _dot(s, w_ref[0].astype(BF16)) + b_ref[0]


def _modulation(c_all, ada_w, ada_b):
    depth = ada_w.shape[0]
    return pl.pallas_call(
        _mod_kernel,
        grid=(depth, N_MOD),
        in_specs=[pl.BlockSpec((8, D_MODEL), lambda l, j: (0, 0)),
                  pl.BlockSpec((1, D_MODEL, D_MODEL), lambda l, j: (l, 0, j)),
                  pl.BlockSpec((1, 1, D_MODEL), lambda l, j: (l, 0, j))],
        out_specs=pl.BlockSpec((1, 8, D_MODEL), lambda l, j: (l, 0, j)),
        out_shape=jax.ShapeDtypeStruct((depth, 8, N_MOD * D_MODEL), F32),
        compiler_params=_cparams(2),
        name="modulation",
    )(c_all, ada_w, ada_b.reshape(depth, 1, N_MOD * D_MODEL))


def _ffn_kernel(*refs, n_first, mix):
    n_src = len(refs) - 7 - (5 if mix else 0)
    h_refs, (mod_ref, wg_ref, wu_ref, wd_ref, g_ref, b_ref, o_ref) = refs[:n_src], refs[-7:]
    off = 3 if mix else 0
    shift, scale, gate = mod_ref[0, off:off + 1, :], mod_ref[0, off + 1:off + 2, :], mod_ref[0, off + 2:off + 3, :]
    for part in range(FFN_ROWS // TM):
        rows = slice(TM * part, TM * (part + 1))
        h = h_refs[0][rows, :]
        if n_src == 2:
            h = jnp.where(pl.program_id(0) < n_first, h, h_refs[1][rows, :])
        if mix:
            a_ref, bm_ref, wo_ref, g1_ref, b1_ref = refs[n_src:n_src + 5]
            half = wo_ref.shape[0] // 2
            ya = (_dot_tn(a_ref[:, rows], wo_ref[0:half, :]) if mix[0] == "t"
                  else _dot(a_ref[rows, :], wo_ref[0:half, :]))
            y_mix = ya + _dot_tn(bm_ref[:, rows], wo_ref[half:2 * half, :])
            h = _layer_norm(DN_ALPHA * h + mod_ref[0, 2:3, :] * y_mix, g1_ref[...], b1_ref[...])
        hm = (h * (1.0 + scale) + shift).astype(BF16)
        y = jnp.zeros((TM, D_MODEL), F32)
        for c in range(len(FF_SPLITS) - 1):
            cols = slice(FF_SPLITS[c], FF_SPLITS[c + 1])
            gt = _dot(hm, wg_ref[:, cols])
            up = _dot(hm, wu_ref[:, cols])
            act = (gt * jax.nn.sigmoid(gt) * up).astype(BF16)
            y = y + _dot(act, wd_ref[cols, :])
        o_ref[rows, :] = _layer_norm(DN_ALPHA * h + (0.5 * gate) * y, g_ref[...], b_ref[...])


def _ffn(h, mods, wg, wu, wd, g, b, n_rows, seq_len, bsz, h_tail=None, mixer=None):
    n_first = h.shape[0] // FFN_ROWS
    mod_index = lambda i: jnp.minimum(i // (seq_len // FFN_ROWS), bsz)
    if h_tail is None:
        sources, src_specs = [h], [pl.BlockSpec((FFN_ROWS, D_MODEL), lambda i: (i, 0))]
    else:
        sources = [h, h_tail]
        src_specs = [pl.BlockSpec((FFN_ROWS, D_MODEL), lambda i: (jnp.minimum(i, n_first - 1), 0)),
                     pl.BlockSpec((FFN_ROWS, D_MODEL), lambda i: (jnp.maximum(i - n_first, 0), 0))]
    layouts = None
    if mixer is not None:
        layouts, mix_a, mix_b, w_out, g1, b1 = mixer
        half = w_out.shape[0] // 2
        t_spec = pl.BlockSpec((half, FFN_ROWS), lambda i: (0, i))
        n_spec = pl.BlockSpec((FFN_ROWS, half), lambda i: (i, 0))
        sources += [mix_a, mix_b, w_out, g1, b1]
        src_specs += [t_spec if layouts[0] == "t" else n_spec, t_spec,
                      _resident((2 * half, D_MODEL), lambda i: (0, 0)),
                      pl.BlockSpec((1, D_MODEL), lambda i: (0, 0)), pl.BlockSpec((1, D_MODEL), lambda i: (0, 0))]
    return pl.pallas_call(
        functools.partial(_ffn_kernel, n_first=n_first, mix=layouts),
        grid=(n_rows // FFN_ROWS,),
        in_specs=src_specs + [
                  pl.BlockSpec((1, mods.shape[1], D_MODEL), lambda i: (mod_index(i), 0, 0)),
                  _resident((D_MODEL, D_FF), lambda i: (0, 0)),
                  _resident((D_MODEL, D_FF), lambda i: (0, 0)),
                  _resident((D_FF, D_MODEL), lambda i: (0, 0)),
                  pl.BlockSpec((1, D_MODEL), lambda i: (0, 0)),
                  pl.BlockSpec((1, D_MODEL), lambda i: (0, 0))],
        out_specs=pl.BlockSpec((FFN_ROWS, D_MODEL), lambda i: (i, 0)),
        out_shape=jax.ShapeDtypeStruct((n_rows, D_MODEL), F32),
        compiler_params=_cparams(1),
        name="ffn_postnorm",
    )(*sources, mods, wg, wu, wd, g, b)


def _ones_rows(n):
    row = lax.broadcasted_iota(jnp.int32, (BF16_ROWS, n), 0)
    return jnp.where(row == 0, 1.0, 0.0).astype(BF16)


def _rope(x, cos, sin_signed, quarter):
    lane = lax.broadcasted_iota(jnp.int32, x.shape, 1)
    first = (lane % (2 * quarter)) < quarter
    rot = jnp.where(first, pltpu.roll(x, LANES - quarter, 1), pltpu.roll(x, quarter, 1))
    return x * cos + rot * sin_signed


EV_QLAT = 0
EV_BQ = 256
EV_KVLAT = 768
EV_KPE = 896
EV_BK = 1024
EV_BV = 1536
EV_PCOLS = 2048


def _even_proj_kernel(h_ref, mod_ref, win_ref, gq_ref, gkv_ref, wuq_ref, wabs_ref,
                      cosa_ref, sina_ref, cosb_ref, sinb_ref,
                      qa_ref, ka_ref, ct_ref, qb_ref, kb_ref, vbt_ref):
    h = h_ref[...]
    hm = (h * (1.0 + mod_ref[0, 1:2, :]) + mod_ref[0, 0:1, :]).astype(BF16)
    p = _dot(hm, win_ref[...])
    cosa, sina, cosb, sinb = cosa_ref[...], sina_ref[...], cosb_ref[...], sinb_ref[...]

    qn = _rms_norm(p[:, EV_QLAT:EV_QLAT + A_Q_RANK], gq_ref[...], RMS_EPS).astype(BF16)
    q = _dot(qn, wuq_ref[...])
    n_nope = A_HEADS * A_NOPE
    qabs = _dot(q[:, :n_nope].astype(BF16), wabs_ref[...])
    qpe = [_rope(q[:, n_nope + LANES * s:n_nope + LANES * (s + 1)], cosa, sina, A_ROPE // 4) for s in range(2)]
    qpe_t = jnp.concatenate(qpe, axis=1).T
    for hd in range(A_HEADS):
        qa_ref[hd, 0:A_KV_RANK, :] = (qabs[:, A_KV_RANK * hd:A_KV_RANK * (hd + 1)].T * A_QSCALE).astype(BF16)
        qa_ref[hd, A_KV_RANK:A_QK, :] = (qpe_t[A_ROPE * hd:A_ROPE * (hd + 1), :] * A_QSCALE).astype(BF16)

    cn = _rms_norm(p[:, EV_KVLAT:EV_KVLAT + A_KV_RANK], gkv_ref[...], RMS_EPS)
    kpe = _rope(p[:, EV_KPE:EV_KPE + LANES], cosa, sina, A_ROPE // 4)
    ka_ref[:, 0:A_KV_RANK] = cn.astype(BF16)
    ka_ref[:, A_KV_RANK:A_QK] = kpe[:, 0:A_ROPE].astype(BF16)
    ones_rows = _ones_rows(TM)
    cnt = jnp.concatenate([cn.T.astype(BF16), ones_rows], axis=0)
    for s in range(TM // VB):
        ct_ref[s] = cnt[:, VB * s:VB * (s + 1)]

    for hd in range(B_HEADS):
        sl = slice(LANES * hd, LANES * (hd + 1))
        qb_ref[hd] = (_rope(p[:, EV_BQ:EV_BQ + B_QK][:, sl], cosb, sinb, B_HD // 4).T * B_QSCALE).astype(BF16)
        kb_ref[hd] = _rope(p[:, EV_BK:EV_BK + B_QK][:, sl], cosb, sinb, B_HD // 4).astype(BF16)
        vt = jnp.concatenate([p[:, EV_BV:EV_BV + B_QK][:, sl].T.astype(BF16), ones_rows], axis=0)
        for s in range(TM // VB):
            vbt_ref[hd, s] = vt[:, VB * s:VB * (s + 1)]


def _even_proj(h, mod3, win, gq, gkv, wuq, wabs, tabs, n_tiles, mod_index, tab_index):
    t_all = n_tiles * TM
    row = lambda i: (i, 0)
    tab = lambda i: (tab_index(i), 0)
    const = lambda i: (0, 0)
    return pl.pallas_call(
        _even_proj_kernel,
        grid=(n_tiles,),
        in_specs=[pl.BlockSpec((TM, D_MODEL), row),
                  pl.BlockSpec((1, 3, D_MODEL), lambda i: (mod_index(i), 0, 0)),
                  _resident((D_MODEL, EV_PCOLS), const),
                  pl.BlockSpec((1, A_Q_RANK), const),
                  pl.BlockSpec((1, A_KV_RANK), const),
                  _resident((A_Q_RANK, A_HEADS * (A_NOPE + A_ROPE)), const),
                  _resident((A_HEADS * A_NOPE, A_HEADS * A_KV_RANK), const),
                  pl.BlockSpec((TM, LANES), tab), pl.BlockSpec((TM, LANES), tab),
                  pl.BlockSpec((TM, LANES), tab), pl.BlockSpec((TM, LANES), tab)],
        out_specs=[pl.BlockSpec((A_HEADS, A_QK, TM), lambda i: (0, 0, i)),
                   pl.BlockSpec((TM, A_QK), row),
                   pl.BlockSpec((TM // VB, DV, VB), lambda i: (i, 0, 0)),
                   pl.BlockSpec((B_HEADS, LANES, TM), lambda i: (0, 0, i)),
                   pl.BlockSpec((B_HEADS, TM, LANES), lambda i: (0, i, 0)),
                   pl.BlockSpec((B_HEADS, TM // VB, DV, VB), lambda i: (0, i, 0, 0))],
        out_shape=[jax.ShapeDtypeStruct((A_HEADS, A_QK, t_all), BF16),
                   jax.ShapeDtypeStruct((t_all, A_QK), BF16),
                   jax.ShapeDtypeStruct((t_all // VB, DV, VB), BF16),
                   jax.ShapeDtypeStruct((B_HEADS, LANES, t_all), BF16),
                   jax.ShapeDtypeStruct((B_HEADS, t_all, LANES), BF16),
                   jax.ShapeDtypeStruct((B_HEADS, t_all // VB, DV, VB), BF16)],
        compiler_params=_cparams(1),
        name="even_proj",
    )(h, mod3, win, gq, gkv, wuq, wabs, *tabs)


ITEMS = 1
N_SLOTS = 8
KEY_GROUP = 2


class _Flash:
    def __init__(self, s_refs, p_refs, acc_ref):
        self.s, self.p, self.acc = s_refs, p_refs, acc_ref
        self.dz = jnp.minimum(pl.program_id(0), 0)

    def rows(self, start, size):
        return pl.ds(pl.multiple_of(self.dz + start, 16), size)

    def qk(self, b, k, q_t, bias=None):
        rows = k.shape[0]
        s = _dot(k, q_t)
        if bias is not None:
            s = s + bias
        self.s[b % N_SLOTS][self.rows(0, rows), :] = s
        return jnp.max(s.reshape(rows // SUBLANES, SUBLANES, NQ), axis=0)

    def sm(self, blocks, rows, maxima, m_old):
        m8 = maxima[0]
        for x in maxima[1:]:
            m8 = jnp.maximum(m8, x)
        m_new = jnp.maximum(m_old, jnp.max(m8, axis=0, keepdims=True))
        alpha = jnp.exp2(m_old - m_new)
        for b in blocks:
            s_ref, p_ref = self.s[b % N_SLOTS], self.p[b % N_SLOTS]
            for r in range(rows // STRIP):
                rr = self.rows(STRIP * r, STRIP)
                p_ref[rr, :] = jnp.exp2(s_ref[rr, :] - m_new).astype(BF16)
        return m_new, alpha

    def pv(self, blocks, vt_lists, alpha):
        pv = None
        for b, vts in zip(blocks, vt_lists):
            for i, vt in enumerate(vts):
                d = _dot(vt, self.p[b % N_SLOTS][self.rows(VB * i, VB), :])
                pv = d if pv is None else pv + d
        self.acc[...] = pv if alpha is None else alpha * self.acc[...] + pv

    def run(self, q_t, key_blocks, group, first=0):
        groups, i = [], 0
        while i < len(key_blocks):
            j = i + 1
            while j < len(key_blocks) and j - i < group and key_blocks[j][0].shape == key_blocks[i][0].shape:
                j += 1
            groups.append(list(range(i, j)))
            i = j
        m = jnp.full((1, NQ), NEG_INF, F32)
        maxima, alphas = {}, {}
        for h in range(len(groups) + 2):
            if h < len(groups):
                for b in groups[h]:
                    blk = key_blocks[b]
                    maxima[b] = self.qk(first + b, blk[0], q_t, blk[2] if len(blk) > 2 else None)
            if 0 <= h - 1 < len(groups):
                g = groups[h - 1]
                m, alphas[h - 1] = self.sm([first + b for b in g], key_blocks[g[0]][0].shape[0],
                                           [maxima[b] for b in g], m)
            if 0 <= h - 2 < len(groups):
                g = groups[h - 2]
                self.pv([first + b for b in g], [key_blocks[b][1] for b in g], alphas[h - 2] if h > 2 else None)

    def normalised(self):
        return self.acc[0:LANES, :] * (1.0 / self.acc[LANES:LANES + 1, :])


def _flash_scratch():
    per_item = ([pltpu.VMEM((TK, NQ), F32)] * N_SLOTS + [pltpu.VMEM((TK, NQ), BF16)] * N_SLOTS
                + [pltpu.VMEM((DV, NQ), F32)])
    return per_item * ITEMS


def _flash_from_scratch(refs):
    per = 2 * N_SLOTS + 1
    return [_Flash(r[0:N_SLOTS], r[N_SLOTS:2 * N_SLOTS], r[2 * N_SLOTS])
            for r in (refs[per * u:per * (u + 1)] for u in range(ITEMS))]


def _block_diag_queries_t(q_t):
    row = lax.broadcasted_iota(jnp.int32, q_t.shape, 0)
    zero = jnp.zeros_like(q_t)
    half = LANES // 2
    return jnp.concatenate([jnp.where(row < half, q_t, zero), jnp.where(row >= half, q_t, zero)], axis=1)


def _mla_kernel(qa_ref, kl_ref, kc_ref, ctl_ref, ctc_ref, wuvt_ref, o_ref, *scratch, n_lat_steps, n_qb_lat):
    fls = _flash_from_scratch(scratch)
    is_latent = pl.program_id(1) < n_qb_lat
    ctx_block = lambda: (kc_ref[...], [ctc_ref[0]])
    lat_block = lambda t: (kl_ref[TK * t:TK * (t + 1), :], [ctl_ref[(TK // VB) * t + i] for i in range(TK // VB)])

    def group(gg, carry):
        gs = [ITEMS * gg + u for u in range(ITEMS)]
        q_ts = [jnp.concatenate([qa_ref[2 * g], qa_ref[2 * g + 1]], axis=1) for g in gs]

        @pl.when(is_latent)
        def _():
            for fl, q_t in zip(fls, q_ts):
                fl.run(q_t, [lat_block(t) for t in range(n_lat_steps)] + [ctx_block()], KEY_GROUP)

        @pl.when(jnp.logical_not(is_latent))
        def _():
            for fl, q_t in zip(fls, q_ts):
                fl.run(q_t, [ctx_block()], KEY_GROUP)

        for fl, g in zip(fls, gs):
            o_lat = fl.normalised().astype(BF16)
            for i in range(2):
                o = _dot(wuvt_ref[2 * g + i], o_lat[:, TQ * i:TQ * (i + 1)])
                o_ref[pl.ds(pl.multiple_of((2 * g + i) * A_V, A_V), A_V), :] = o.astype(BF16)
        return carry

    lax.fori_loop(0, A_HEADS // 2 // ITEMS, group, 0)


def _mla(qa, ka, ct, wuvt, bsz, n, m):
    t_lat = bsz * n
    t_all = t_lat + bsz * m
    nqb = n // TQ
    qblk = lambda b, j: jnp.where(j < nqb, b * nqb + j, t_lat // TQ + b)
    kern = functools.partial(_mla_kernel, n_lat_steps=n // TK, n_qb_lat=nqb)
    return pl.pallas_call(
        kern,
        grid=(bsz, nqb + 1),
        in_specs=[pl.BlockSpec((A_HEADS, A_QK, TQ), lambda b, j: (0, 0, qblk(b, j))),
                  _resident((n, A_QK), lambda b, j: (b, 0)),
                  pl.BlockSpec((m, A_QK), lambda b, j: (t_lat // m + b, 0)),
                  _resident((n // VB, DV, VB), lambda b, j: (b, 0, 0)),
                  pl.BlockSpec((m // VB, DV, VB), lambda b, j: (t_lat // m + b, 0, 0)),
                  pl.BlockSpec((A_HEADS, A_V, A_KV_RANK), lambda b, j: (0, 0, 0))],
        out_specs=pl.BlockSpec((A_HEADS * A_V, TQ), lambda b, j: (0, qblk(b, j))),
        out_shape=jax.ShapeDtypeStruct((A_HEADS * A_V, t_all), BF16),
        scratch_shapes=_flash_scratch(),
        compiler_params=_cparams(2),
        name="mla_attention",
    )(qa, ka, ka, ct, ct, wuvt)


def _diff_kernel(qb_ref, kl_ref, kc_ref, vl_ref, vc_ref, lam_ref, gsub_ref, o_ref, *scratch, n_lat_steps, n_qb_lat,
                 lam_init):
    fls = _flash_from_scratch(scratch)
    is_latent = pl.program_id(1) < n_qb_lat
    lv = lam_ref[...]
    lam = (jnp.exp(jnp.sum(lv[0:1] * lv[1:2], axis=1, keepdims=True))
           - jnp.exp(jnp.sum(lv[2:3] * lv[3:4], axis=1, keepdims=True)) + lam_init)

    def group(gg, carry):
        hds = [ITEMS * gg + u for u in range(ITEMS)]
        q_ts = [_block_diag_queries_t(qb_ref[hd]) for hd in hds]
        ctx_block = lambda hd: (kc_ref[hd], [vc_ref[hd, 0]])
        lat_block = lambda hd, t: (kl_ref[hd, TK * t:TK * (t + 1), :],
                                   [vl_ref[hd, (TK // VB) * t + i] for i in range(TK // VB)])

        @pl.when(is_latent)
        def _():
            for fl, q_t, hd in zip(fls, q_ts, hds):
                fl.run(q_t, [lat_block(hd, t) for t in range(n_lat_steps)] + [ctx_block(hd)], KEY_GROUP)

        @pl.when(jnp.logical_not(is_latent))
        def _():
            for fl, q_t, hd in zip(fls, q_ts, hds):
                fl.run(q_t, [ctx_block(hd)], KEY_GROUP)

        for fl, hd in zip(fls, hds):
            on = fl.normalised()
            o = on[:, 0:TQ] - lam * on[:, TQ:2 * TQ]
            o = o * lax.rsqrt(jnp.mean(o * o, axis=0, keepdims=True) + SUBLN_EPS) * gsub_ref[...]
            o_ref[pl.ds(pl.multiple_of(hd * LANES, LANES), LANES), :] = (o * (1.0 - lam_init)).astype(BF16)
        return carry

    lax.fori_loop(0, B_HEADS // ITEMS, group, 0)


def _diff(qb, kb, vbt, lam_vec, gsub_col, lam_init, bsz, n, m):
    t_lat = bsz * n
    t_all = t_lat + bsz * m
    nqb = n // TQ
    qblk = lambda b, j: jnp.where(j < nqb, b * nqb + j, t_lat // TQ + b)
    kern = functools.partial(_diff_kernel, n_lat_steps=n // TK, n_qb_lat=nqb, lam_init=lam_init)
    return pl.pallas_call(
        kern,
        grid=(bsz, nqb + 1),
        in_specs=[pl.BlockSpec((B_HEADS, LANES, TQ), lambda b, j: (0, 0, qblk(b, j))),
                  _resident((B_HEADS, n, LANES), lambda b, j: (0, b, 0)),
                  pl.BlockSpec((B_HEADS, m, LANES), lambda b, j: (0, t_lat // m + b, 0)),
                  _resident((B_HEADS, n // VB, DV, VB), lambda b, j: (0, b, 0, 0)),
                  pl.BlockSpec((B_HEADS, m // VB, DV, VB), lambda b, j: (0, t_lat // m + b, 0, 0)),
                  pl.BlockSpec((4, B_HD), lambda b, j: (0, 0)),
                  pl.BlockSpec((2 * B_HD, 1), lambda b, j: (0, 0))],
        out_specs=pl.BlockSpec((B_HEADS * LANES, TQ), lambda b, j: (0, qblk(b, j))),
        out_shape=jax.ShapeDtypeStruct((B_HEADS * LANES, t_all), BF16),
        scratch_shapes=_flash_scratch(),
        compiler_params=_cparams(2),
        name="diff_attention",
    )(qb, kb, kb, vbt, vbt, lam_vec, gsub_col)


def _odd_proj_kernel(h_ref, mod_ref, win_ref, u_ref, q_ref, k_ref, vt_ref):
    h = h_ref[...]
    hm = (h * (1.0 + mod_ref[0, 1:2, :]) + mod_ref[0, 0:1, :]).astype(BF16)
    p = _dot(hm, win_ref[...])
    u_ref[...] = p[:, 0:C_WIDTH]
    ones_rows = _ones_rows(TM)
    for s in range(D_WIDTH // LANES):
        sl = slice(LANES * s, LANES * (s + 1))
        q_ref[s] = (p[:, C_WIDTH:C_WIDTH + D_WIDTH][:, sl].T * D_QSCALE).astype(BF16)
        k_ref[s] = p[:, C_WIDTH + D_WIDTH:C_WIDTH + 2 * D_WIDTH][:, sl].astype(BF16)
        vt = p[:, C_WIDTH + 2 * D_WIDTH:C_WIDTH + 3 * D_WIDTH][:, sl].T.astype(BF16)
        vt = jnp.concatenate([vt, ones_rows], axis=0)
        for i in range(TM // VB):
            vt_ref[s, i] = vt[:, VB * i:VB * (i + 1)]


def _odd_proj(h, mod3, win, n_tiles, mod_index):
    t_all = n_tiles * TM
    slabs = D_WIDTH // LANES
    return pl.pallas_call(
        _odd_proj_kernel,
        grid=(n_tiles,),
        in_specs=[pl.BlockSpec((TM, D_MODEL), lambda i: (i, 0)),
                  pl.BlockSpec((1, 3, D_MODEL), lambda i: (mod_index(i), 0, 0)),
                  _resident((D_MODEL, C_WIDTH + 3 * D_WIDTH), lambda i: (0, 0))],
        out_specs=[pl.BlockSpec((TM, C_WIDTH), lambda i: (i, 0)),
                   pl.BlockSpec((slabs, LANES, TM), lambda i: (0, 0, i)),
                   pl.BlockSpec((slabs, TM, LANES), lambda i: (0, i, 0)),
                   pl.BlockSpec((slabs, TM // VB, DV, VB), lambda i: (0, i, 0, 0))],
        out_shape=[jax.ShapeDtypeStruct((t_all, C_WIDTH), F32),
                   jax.ShapeDtypeStruct((slabs, LANES, t_all), BF16),
                   jax.ShapeDtypeStruct((slabs, t_all, LANES), BF16),
                   jax.ShapeDtypeStruct((slabs, t_all // VB, DV, VB), BF16)],
        compiler_params=_cparams(1),
        name="odd_proj",
    )(h, mod3, win)


def _pool_kernel(prev_ref, u_ref, next_ref, w_ref, ps_ref, o_ref, ext_ref, *, tiles_per_seq):
    t = pl.program_id(0) % tiles_per_seq
    n = tiles_per_seq * TM
    ext_ref[0:POOL_HALO, :] = jnp.where(t > 0, prev_ref[...], 0.0)
    ext_ref[POOL_HALO:POOL_HALO + TM, :] = u_ref[...]
    ext_ref[POOL_HALO + TM:2 * POOL_HALO + TM, :] = jnp.where(t < tiles_per_seq - 1, next_ref[...], 0.0)
    pos = t * TM + lax.broadcasted_iota(jnp.int32, (TM, 1), 0)
    for g, w in enumerate(C_WINDOWS):
        left = w // 2
        right = w - 1 - left
        lanes = slice(C_GW * g, C_GW * (g + 1))
        tot = ext_ref[POOL_HALO - left:POOL_HALO - left + TM, lanes]
        for k in range(1 - left, right + 1):
            tot = tot + ext_ref[POOL_HALO + k:POOL_HALO + k + TM, lanes]
        cnt = jnp.minimum(pos + right + 1, n) - jnp.maximum(pos - left, 0)
        pooled = tot / cnt.astype(F32) - u_ref[:, lanes]
        y = _dot(pooled.astype(BF16), w_ref[g])
        o_ref[:, lanes] = (y * ps_ref[:, lanes]).astype(BF16)


def _pool(u, w_pool, pool_scale, n_tiles, tiles_per_seq):
    per = TM // POOL_HALO
    last = u.shape[0] // POOL_HALO - 1
    return pl.pallas_call(
        functools.partial(_pool_kernel, tiles_per_seq=tiles_per_seq),
        grid=(n_tiles,),
        in_specs=[pl.BlockSpec((POOL_HALO, C_WIDTH), lambda i: (jnp.maximum(i * per - 1, 0), 0)),
                  pl.BlockSpec((TM, C_WIDTH), lambda i: (i, 0)),
                  pl.BlockSpec((POOL_HALO, C_WIDTH), lambda i: (jnp.minimum((i + 1) * per, last), 0)),
                  pl.BlockSpec((C_GROUPS, C_GW, C_GW), lambda i: (0, 0, 0)),
                  pl.BlockSpec((1, C_WIDTH), lambda i: (0, 0))],
        out_specs=pl.BlockSpec((TM, C_WIDTH), lambda i: (i, 0)),
        out_shape=jax.ShapeDtypeStruct((n_tiles * TM, C_WIDTH), BF16),
        scratch_shapes=[pltpu.VMEM((TM + 2 * POOL_HALO, C_WIDTH), F32)],
        compiler_params=_cparams(1),
        name="multiscale_pool",
    )(u, u, u, w_pool, pool_scale)


def _na_row_table(rows):
    qr_per = TQ // GRID_W
    nblk = rows // qr_per
    g_of = (0, 1, nblk - 1)
    gb_of = tuple(min(max(g - 1, 0), nblk - 3) for g in g_of)
    table = {}
    for p in range(NA_PATTERNS):
        for kr in range(NA_KEY_ROWS):
            for qr in range(qr_per):
                r = qr_per * g_of[p] + qr
                ka = qr_per * gb_of[p] + kr
                r0 = min(max(r - NA_ROWS // 2, 0), rows - NA_ROWS)
                table[p, kr, qr] = (r0 <= ka < r0 + NA_ROWS, ka - r + NA_ROWS - 1)
    return table


def _na_bias_kernel(rpb_ref, o_ref, *, rows):
    hd = pl.program_id(0)
    n_r, n_c = 2 * NA_ROWS - 1, 2 * NA_COLS - 1
    qr_per = TQ // GRID_W
    kc = lax.broadcasted_iota(jnp.int32, (GRID_W, TQ), 0)
    q = lax.broadcasted_iota(jnp.int32, (GRID_W, TQ), 1)
    qc, qrow = q % GRID_W, q // GRID_W
    idx = kc - qc + NA_COLS - 1
    cs = jnp.clip(qc - NA_COLS // 2, 0, GRID_W - NA_COLS)
    col_ok = (kc >= cs) & (kc < cs + NA_COLS)
    neg = jnp.full((GRID_W, TQ), NEG_INF, F32)
    toeplitz = []
    for dr in range(n_r):
        acc = neg
        for j in range(n_c):
            acc = jnp.where(idx == j, rpb_ref[hd * n_r * n_c + dr * n_c + j], acc)
        toeplitz.append(jnp.where(col_ok, acc * LOG2E, neg))
    table = _na_row_table(rows)
    for p in range(NA_PATTERNS):
        for kr in range(NA_KEY_ROWS):
            blk = neg
            for qr in range(qr_per):
                ok, ridx = table[p, kr, qr]
                if ok:
                    blk = jnp.where(qrow == qr, toeplitz[ridx], blk)
            o_ref[p, 0, GRID_W * kr:GRID_W * (kr + 1), :] = blk


def _na_bias(rpb, rows):
    n_r, n_c = 2 * NA_ROWS - 1, 2 * NA_COLS - 1
    n_keys = NA_KEY_ROWS * GRID_W
    return pl.pallas_call(
        functools.partial(_na_bias_kernel, rows=rows),
        grid=(D_HEADS,),
        in_specs=[pl.BlockSpec(memory_space=pltpu.SMEM)],
        out_specs=pl.BlockSpec((NA_PATTERNS, 1, n_keys, TQ), lambda i: (0, i // 2, 0, i % 2)),
        out_shape=jax.ShapeDtypeStruct((NA_PATTERNS, D_HEADS // 2, n_keys, 2 * TQ), F32),
        compiler_params=_cparams(1),
        name="na_bias",
    )(rpb.reshape(D_HEADS * n_r * n_c))


def _na_kernel(q_ref, kl_ref, kc_ref, vl_ref, vc_ref, bias_ref, o_ref, *scratch, nqb):
    s_refs, p_refs, accs = scratch[0:N_SLOTS], scratch[N_SLOTS:2 * N_SLOTS], scratch[2 * N_SLOTS:]
    n_blk = NA_KEY_ROWS * GRID_W // VB
    half = LANES // 2
    for i in range(NA_GS):
        fl = _Flash(s_refs, p_refs, accs[i % len(accs)])
        g = pl.program_id(2) * NA_GS + i
        gb = jnp.clip(g - 1, 0, nqb - n_blk)
        pat = jnp.where(g == 0, 0, jnp.where(g == nqb - 1, 2, 1))
        cols = slice(TQ * i, TQ * (i + 1))
        q_t = _block_diag_queries_t(q_ref[0, :, cols])
        blocks = [(kl_ref[0, pl.ds(pl.multiple_of((gb + j) * VB, VB), VB), :], [vl_ref[0, gb + j]],
                   bias_ref[pat, 0, VB * j:VB * (j + 1), :]) for j in range(n_blk)]
        blocks.append((kc_ref[0], [vc_ref[0, 0]]))
        fl.run(q_t, blocks, len(blocks), first=len(blocks) * i)
        on = fl.normalised()
        o_ref[0:half, cols] = on[0:half, 0:TQ].astype(BF16)
        o_ref[half:LANES, cols] = on[half:LANES, TQ:2 * TQ].astype(BF16)


def _natten(qn, kn, vnt, bias, bsz, n, m):
    t_lat = bsz * n
    nqb = n // TQ
    steps = nqb // NA_GS
    slabs = D_WIDTH // LANES
    return pl.pallas_call(
        functools.partial(_na_kernel, nqb=nqb),
        grid=(bsz, slabs, steps),
        in_specs=[pl.BlockSpec((1, LANES, NA_GS * TQ), lambda b, s, g: (s, 0, b * steps + g)),
                  pl.BlockSpec((1, n, LANES), lambda b, s, g: (s, b, 0)),
                  pl.BlockSpec((1, m, LANES), lambda b, s, g: (s, t_lat // m + b, 0)),
                  pl.BlockSpec((1, n // VB, DV, VB), lambda b, s, g: (s, b, 0, 0)),
                  pl.BlockSpec((1, m // VB, DV, VB), lambda b, s, g: (s, t_lat // m + b, 0, 0)),
                  pl.BlockSpec((NA_PATTERNS, 1, NA_KEY_ROWS * GRID_W, 2 * TQ), lambda b, s, g: (0, s, 0, 0))],
        out_specs=pl.BlockSpec((LANES, NA_GS * TQ), lambda b, s, g: (s, b * steps + g)),
        out_shape=jax.ShapeDtypeStruct((D_WIDTH, t_lat), BF16),
        scratch_shapes=([pltpu.VMEM((VB, NQ), F32)] * N_SLOTS + [pltpu.VMEM((VB, NQ), BF16)] * N_SLOTS
                        + [pltpu.VMEM((DV, NQ), F32)] * 2),
        compiler_params=_cparams(3),
        name="neighbourhood_attention",
    )(qn, kn, kn, vnt, vnt, bias)


def _rope_tables(n, rot_dim):
    t = np.arange(n)
    row = (t // GRID_W).astype(np.float32)
    col = (t % GRID_W).astype(np.float32)
    quarter = rot_dim // 4
    inv = jnp.asarray(ROPE_BASE, F32) ** (-jnp.arange(quarter, dtype=F32) / quarter)
    ang_r = jnp.asarray(row)[:, None] * inv[None, :]
    ang_c = jnp.asarray(col)[:, None] * inv[None, :]
    ang = jnp.concatenate([ang_r, ang_r, ang_c, ang_c], -1)
    sign = np.where((np.arange(rot_dim) % (2 * quarter)) < quarter, -1.0, 1.0).astype(np.float32)
    cos = jnp.tile(jnp.cos(ang), (1, LANES // rot_dim))
    sin = jnp.tile(jnp.sin(ang) * sign[None, :], (1, LANES // rot_dim))
    cos = jnp.concatenate([cos, jnp.ones((TM, LANES), F32)], 0)
    sin = jnp.concatenate([sin, jnp.zeros((TM, LANES), F32)], 0)
    return cos, sin


def _even_weights(w_in, w_uq, w_ukv):
    o = A_Q_RANK + B_QK
    kpe = jnp.pad(w_in[:, o + A_KV_RANK:o + A_KV_RANK + A_ROPE], ((0, 0), (0, LANES - A_ROPE)))
    win = jnp.concatenate([w_in[:, 0:A_Q_RANK], w_in[:, A_Q_RANK:o], w_in[:, o:o + A_KV_RANK], kpe,
                           w_in[:, o + A_KV_RANK + A_ROPE:o + A_KV_RANK + A_ROPE + B_QK],
                           w_in[:, o + A_KV_RANK + A_ROPE + B_QK:]], axis=1).astype(BF16)
    uq = w_uq.reshape(A_Q_RANK, A_HEADS, A_NOPE + A_ROPE)
    wuq = jnp.concatenate([uq[:, :, :A_NOPE].reshape(A_Q_RANK, A_HEADS * A_NOPE),
                           uq[:, :, A_NOPE:].reshape(A_Q_RANK, A_HEADS * A_ROPE)], axis=1).astype(BF16)
    ukv = w_ukv.reshape(A_KV_RANK, A_HEADS, A_NOPE + A_V)
    uk_t = jnp.transpose(ukv[:, :, :A_NOPE], (1, 2, 0))
    wabs = jnp.zeros((A_HEADS, A_NOPE, A_HEADS, A_KV_RANK), F32)
    wabs = wabs.at[np.arange(A_HEADS), :, np.arange(A_HEADS), :].set(uk_t)
    wabs = wabs.reshape(A_HEADS * A_NOPE, A_HEADS * A_KV_RANK).astype(BF16)
    wuvt = jnp.transpose(ukv[:, :, A_NOPE:], (1, 2, 0)).astype(BF16)
    return win, wuq, wabs, wuvt


def kernel(x, c, ctx, c_ctx, ada_w, ada_b, ln_g, ln_b, ffn_w_gate, ffn_w_up, ffn_w_down, ev_w_in, ev_w_out,
           ev_g_qlat, ev_g_kvlat, ev_w_uq, ev_w_ukv, ev_lam, ev_g_sub, od_w_in, od_w_out, od_w_pool,
           od_pool_scale, od_rpb):
    bsz, n, d = x.shape
    m = ctx.shape[1]
    assert d == D_MODEL and ada_w.shape[0] == DEPTH == 2
    assert m == TQ and n % FFN_ROWS == 0 and n % GRID_W == 0 and (bsz * m) % FFN_ROWS == 0 and bsz + 1 <= 8
    assert n // TQ >= 3 and (n // TQ) % NA_GS == 0
    t_lat = bsz * n
    t_all = t_lat + bsz * m
    tiles_lat, tiles_all, tiles_seq = t_lat // TM, t_all // TM, n // TM
    mod_index = lambda i: jnp.minimum(i // tiles_seq, bsz)
    tab_index = lambda i: jnp.where(i < tiles_lat, i % tiles_seq, tiles_seq)

    c_all = jnp.zeros((8, D_MODEL), F32).at[:bsz].set(c).at[bsz].set(c_ctx)
    mod = _modulation(c_all, ada_w, ada_b).reshape(DEPTH, 8, N_MOD, D_MODEL)
    row = lambda v: v.reshape(1, -1)
    wg, wu, wd = ffn_w_gate.astype(BF16), ffn_w_up.astype(BF16), ffn_w_down.astype(BF16)

    h = _ffn(x.reshape(t_lat, D_MODEL), mod[0, :, 0:3], wg[0, 0], wu[0, 0], wd[0, 0], row(ln_g[0, 0]), row(ln_b[0, 0]),
             t_all, n, bsz, h_tail=ctx.reshape(bsz * m, D_MODEL))
    win, wuq, wabs, wuvt = _even_weights(ev_w_in[0], ev_w_uq[0], ev_w_ukv[0])
    tabs = _rope_tables(n, A_ROPE) + _rope_tables(n, B_HD)
    qa, ka, ct, qb, kb, vbt = _even_proj(h, mod[0, :, 3:6], win, row(ev_g_qlat[0]), row(ev_g_kvlat[0]), wuq, wabs,
                                         tabs, tiles_all, mod_index, tab_index)
    lam_init = 0.8 - 0.6 * math.exp(-0.3 * 0)
    mix_a = _mla(qa, ka, ct, wuvt, bsz, n, m)
    mix_b = _diff(qb, kb, vbt, ev_lam[0], ev_g_sub[0].reshape(2 * B_HD, 1), lam_init, bsz, n, m)
    h = _ffn(h, mod[0, :, 3:9], wg[0, 1], wu[0, 1], wd[0, 1], row(ln_g[0, 2]), row(ln_b[0, 2]), t_all, n, bsz,
             mixer=("tt", mix_a, mix_b, ev_w_out[0].astype(BF16), row(ln_g[0, 1]), row(ln_b[0, 1])))

    h = _ffn(h, mod[1, :, 0:3], wg[1, 0], wu[1, 0], wd[1, 0], row(ln_g[1, 0]), row(ln_b[1, 0]), t_all, n, bsz)
    u, qn, kn, vnt = _odd_proj(h, mod[1, :, 3:6], od_w_in[0].astype(BF16), tiles_all, mod_index)
    y_pool = _pool(u, od_w_pool[0].astype(BF16), row(od_pool_scale[0]), tiles_lat, tiles_seq)
    bias = _na_bias(od_rpb[0], n // GRID_W)
    y_na = _natten(qn, kn, vnt, bias, bsz, n, m)
    h = _ffn(h, mod[1, :, 3:9], wg[1, 1], wu[1, 1], wd[1, 1], row(ln_g[1, 2]), row(ln_b[1, 2]), t_lat, n, bsz,
             mixer=("nt", y_pool, y_na, od_w_out[0].astype(BF16), row(ln_g[1, 1]), row(ln_b[1, 1])))
    return h.reshape(bsz, n, D_MODEL)
```

```python
import functools
import math

import numpy as np
import jax
import jax.numpy as jnp
from jax import lax
from jax.experimental import pallas as pl
from jax.experimental.pallas import tpu as pltpu

F32 = jnp.float32
BF16 = jnp.bfloat16

D_MODEL = 1024
DEPTH = 2
GRID_W = 64
N_MOD = 9
DN_ALPHA = float((2 * DEPTH) ** 0.25)
LN_EPS = 1e-6
RMS_EPS = 1e-6
SUBLN_EPS = 1e-5
ROPE_BASE = 10000.0
NEG_INF = -1e30
D_FF = 2816

A_HEADS = 8
A_NOPE = 64
A_ROPE = 32
A_V = 64
A_Q_RANK = 256
A_KV_RANK = 128
A_SCALE = float((A_NOPE + A_ROPE) ** -0.5)
A_QK = A_KV_RANK + A_ROPE

B_HEADS = 4
B_HD = 64
B_SCALE = float(B_HD ** -0.5)
B_QK = B_HEADS * 2 * B_HD

C_GROUPS = 4
C_WINDOWS = (2, 4, 8, 16)
C_GW = 128
C_WIDTH = C_GROUPS * C_GW

D_HEADS = 8
D_HD = 64
D_SCALE = float(D_HD ** -0.5)
D_WIDTH = D_HEADS * D_HD
NA_ROWS = 8
NA_COLS = 16

LOG2E = math.log2(math.e)
A_QSCALE = A_SCALE * LOG2E
B_QSCALE = B_SCALE * LOG2E
D_QSCALE = D_SCALE * LOG2E

LANES = 128
SUBLANES = 8
TM = 512
FFN_ROWS = 2 * TM
TQ = 256
NQ = 2 * TQ
TK = 512
VB = 256
STRIP = 64
BF16_ROWS = 16
DV = LANES + BF16_ROWS
MXU_DIM = 256
FF_SPLITS = (0, 6 * MXU_DIM, D_FF)
POOL_HALO = 8
NA_KEY_ROWS = 12
NA_PATTERNS = 3
NA_GS = 8
VMEM_LIMIT = 56 * 1024 * 1024


def _cparams(n_axes):
    return pltpu.CompilerParams(dimension_semantics=("arbitrary",) * n_axes,
                                vmem_limit_bytes=VMEM_LIMIT)


def _resident(shape, index_map):
    return pl.BlockSpec(shape, index_map, pipeline_mode=pl.Buffered(1))


def _layer_norm(z, g, b):
    mu = jnp.mean(z, axis=-1, keepdims=True)
    zc = z - mu
    var = jnp.mean(zc * zc, axis=-1, keepdims=True)
    return zc * lax.rsqrt(var + LN_EPS) * g + b


def _rms_norm(x, g, eps):
    return x * lax.rsqrt(jnp.mean(x * x, axis=-1, keepdims=True) + eps) * g


def _dot(a, b):
    return jnp.dot(a, b, preferred_element_type=F32)


def _dot_tn(a, b):
    return lax.dot_general(a, b, (((0,), (0,)), ((), ())), preferred_element_type=F32)


def _mod_kernel(c_ref, w_ref, b_ref, o_ref):
    x = c_ref[...]
    s = (x * jax.nn.sigmoid(x)).astype(BF16)
    o_ref[0] = _dot(s, w_ref[0].astype(BF16)) + b_ref[0]


def _modulation(c_all, ada_w, ada_b):
    depth = ada_w.shape[0]
    return pl.pallas_call(
        _mod_kernel,
        grid=(depth, N_MOD),
        in_specs=[pl.BlockSpec((8, D_MODEL), lambda l, j: (0, 0)),
                  pl.BlockSpec((1, D_MODEL, D_MODEL), lambda l, j: (l, 0, j)),
                  pl.BlockSpec((1, 1, D_MODEL), lambda l, j: (l, 0, j))],
        out_specs=pl.BlockSpec((1, 8, D_MODEL), lambda l, j: (l, 0, j)),
        out_shape=jax.ShapeDtypeStruct((depth, 8, N_MOD * D_MODEL), F32),
        compiler_params=_cparams(2),
        name="modulation",
    )(c_all, ada_w, ada_b.reshape(depth, 1, N_MOD * D_MODEL))


def _ffn_kernel(*refs, n_first, mix):
    n_src = len(refs) - 7 - (5 if mix else 0)
    h_refs, (mod_ref, wg_ref, wu_ref, wd_ref, g_ref, b_ref, o_ref) = refs[:n_src], refs[-7:]
    off = 3 if mix else 0
    shift, scale, gate = mod_ref[0, off:off + 1, :], mod_ref[0, off + 1:off + 2, :], mod_ref[0, off + 2:off + 3, :]
    for part in range(FFN_ROWS // TM):
        rows = slice(TM * part, TM * (part + 1))
        h = h_refs[0][rows, :]
        if n_src == 2:
            h = jnp.where(pl.program_id(0) < n_first, h, h_refs[1][rows, :])
        if mix:
            a_ref, bm_ref, wo_ref, g1_ref, b1_ref = refs[n_src:n_src + 5]
            half = wo_ref.shape[0] // 2
            ya = (_dot_tn(a_ref[:, rows], wo_ref[0:half, :]) if mix[0] == "t"
                  else _dot(a_ref[rows, :], wo_ref[0:half, :]))
            y_mix = ya + _dot_tn(bm_ref[:, rows], wo_ref[half:2 * half, :])
            h = _layer_norm(DN_ALPHA * h + mod_ref[0, 2:3, :] * y_mix, g1_ref[...], b1_ref[...])
        hm = (h * (1.0 + scale) + shift).astype(BF16)
        y = jnp.zeros((TM, D_MODEL), F32)
        for c in range(len(FF_SPLITS) - 1):
            cols = slice(FF_SPLITS[c], FF_SPLITS[c + 1])
            gt = _dot(hm, wg_ref[:, cols])
            up = _dot(hm, wu_ref[:, cols])
            act = (gt * jax.nn.sigmoid(gt) * up).astype(BF16)
            y = y + _dot(act, wd_ref[cols, :])
        o_ref[rows, :] = _layer_norm(DN_ALPHA * h + (0.5 * gate) * y, g_ref[...], b_ref[...])


def _ffn(h, mods, wg, wu, wd, g, b, n_rows, seq_len, bsz, h_tail=None, mixer=None):
    n_first = h.shape[0] // FFN_ROWS
    mod_index = lambda i: jnp.minimum(i // (seq_len // FFN_ROWS), bsz)
    if h_tail is None:
        sources, src_specs = [h], [pl.BlockSpec((FFN_ROWS, D_MODEL), lambda i: (i, 0))]
    else:
        sources = [h, h_tail]
        src_specs = [pl.BlockSpec((FFN_ROWS, D_MODEL), lambda i: (jnp.minimum(i, n_first - 1), 0)),
                     pl.BlockSpec((FFN_ROWS, D_MODEL), lambda i: (jnp.maximum(i - n_first, 0), 0))]
    layouts = None
    if mixer is not None:
        layouts, mix_a, mix_b, w_out, g1, b1 = mixer
        half = w_out.shape[0] // 2
        t_spec = pl.BlockSpec((half, FFN_ROWS), lambda i: (0, i))
        n_spec = pl.BlockSpec((FFN_ROWS, half), lambda i: (i, 0))
        sources += [mix_a, mix_b, w_out, g1, b1]
        src_specs += [t_spec if layouts[0] == "t" else n_spec, t_spec,
                      _resident((2 * half, D_MODEL), lambda i: (0, 0)),
                      pl.BlockSpec((1, D_MODEL), lambda i: (0, 0)), pl.BlockSpec((1, D_MODEL), lambda i: (0, 0))]
    return pl.pallas_call(
        functools.partial(_ffn_kernel, n_first=n_first, mix=layouts),
        grid=(n_rows // FFN_ROWS,),
        in_specs=src_specs + [
                  pl.BlockSpec((1, mods.shape[1], D_MODEL), lambda i: (mod_index(i), 0, 0)),
                  _resident((D_MODEL, D_FF), lambda i: (0, 0)),
                  _resident((D_MODEL, D_FF), lambda i: (0, 0)),
                  _resident((D_FF, D_MODEL), lambda i: (0, 0)),
                  pl.BlockSpec((1, D_MODEL), lambda i: (0, 0)),
                  pl.BlockSpec((1, D_MODEL), lambda i: (0, 0))],
        out_specs=pl.BlockSpec((FFN_ROWS, D_MODEL), lambda i: (i, 0)),
        out_shape=jax.ShapeDtypeStruct((n_rows, D_MODEL), F32),
        compiler_params=_cparams(1),
        name="ffn_postnorm",
    )(*sources, mods, wg, wu, wd, g, b)


def _ones_rows(n):
    row = lax.broadcasted_iota(jnp.int32, (BF16_ROWS, n), 0)
    return jnp.where(row == 0, 1.0, 0.0).astype(BF16)


def _rope(x, cos, sin_signed, quarter):
    lane = lax.broadcasted_iota(jnp.int32, x.shape, 1)
    first = (lane % (2 * quarter)) < quarter
    rot = jnp.where(first, pltpu.roll(x, LANES - quarter, 1), pltpu.roll(x, quarter, 1))
    return x * cos + rot * sin_signed


EV_QLAT = 0
EV_BQ = 256
EV_KVLAT = 768
EV_KPE = 896
EV_BK = 1024
EV_BV = 1536
EV_PCOLS = 2048


def _even_proj_kernel(h_ref, mod_ref, win_ref, gq_ref, gkv_ref, wuq_ref, wabs_ref,
                      cosa_ref, sina_ref, cosb_ref, sinb_ref,
                      qa_ref, ka_ref, ct_ref, qb_ref, kb_ref, vbt_ref):
    h = h_ref[...]
    hm = (h * (1.0 + mod_ref[0, 1:2, :]) + mod_ref[0, 0:1, :]).astype(BF16)
    p = _dot(hm, win_ref[...])
    cosa, sina, cosb, sinb = cosa_ref[...], sina_ref[...], cosb_ref[...], sinb_ref[...]

    qn = _rms_norm(p[:, EV_QLAT:EV_QLAT + A_Q_RANK], gq_ref[...], RMS_EPS).astype(BF16)
    q = _dot(qn, wuq_ref[...])
    n_nope = A_HEADS * A_NOPE
    qabs = _dot(q[:, :n_nope].astype(BF16), wabs_ref[...])
    qpe = [_rope(q[:, n_nope + LANES * s:n_nope + LANES * (s + 1)], cosa, sina, A_ROPE // 4) for s in range(2)]
    qpe_t = jnp.concatenate(qpe, axis=1).T
    for hd in range(A_HEADS):
        qa_ref[hd, 0:A_KV_RANK, :] = (qabs[:, A_KV_RANK * hd:A_KV_RANK * (hd + 1)].T * A_QSCALE).astype(BF16)
        qa_ref[hd, A_KV_RANK:A_QK, :] = (qpe_t[A_ROPE * hd:A_ROPE * (hd + 1), :] * A_QSCALE).astype(BF16)

    cn = _rms_norm(p[:, EV_KVLAT:EV_KVLAT + A_KV_RANK], gkv_ref[...], RMS_EPS)
    kpe = _rope(p[:, EV_KPE:EV_KPE + LANES], cosa, sina, A_ROPE // 4)
    ka_ref[:, 0:A_KV_RANK] = cn.astype(BF16)
    ka_ref[:, A_KV_RANK:A_QK] = kpe[:, 0:A_ROPE].astype(BF16)
    ones_rows = _ones_rows(TM)
    cnt = jnp.concatenate([cn.T.astype(BF16), ones_rows], axis=0)
    for s in range(TM // VB):
        ct_ref[s] = cnt[:, VB * s:VB * (s + 1)]

    for hd in range(B_HEADS):
        sl = slice(LANES * hd, LANES * (hd + 1))
        qb_ref[hd] = (_rope(p[:, EV_BQ:EV_BQ + B_QK][:, sl], cosb, sinb, B_HD // 4).T * B_QSCALE).astype(BF16)
        kb_ref[hd] = _rope(p[:, EV_BK:EV_BK + B_QK][:, sl], cosb, sinb, B_HD // 4).astype(BF16)
        vt = jnp.concatenate([p[:, EV_BV:EV_BV + B_QK][:, sl].T.astype(BF16), ones_rows], axis=0)
        for s in range(TM // VB):
            vbt_ref[hd, s] = vt[:, VB * s:VB * (s + 1)]


def _even_proj(h, mod3, win, gq, gkv, wuq, wabs, tabs, n_tiles, mod_index, tab_index):
    t_all = n_tiles * TM
    row = lambda i: (i, 0)
    tab = lambda i: (tab_index(i), 0)
    const = lambda i: (0, 0)
    return pl.pallas_call(
        _even_proj_kernel,
        grid=(n_tiles,),
        in_specs=[pl.BlockSpec((TM, D_MODEL), row),
                  pl.BlockSpec((1, 3, D_MODEL), lambda i: (mod_index(i), 0, 0)),
                  _resident((D_MODEL, EV_PCOLS), const),
                  pl.BlockSpec((1, A_Q_RANK), const),
                  pl.BlockSpec((1, A_KV_RANK), const),
                  _resident((A_Q_RANK, A_HEADS * (A_NOPE + A_ROPE)), const),
                  _resident((A_HEADS * A_NOPE, A_HEADS * A_KV_RANK), const),
                  pl.BlockSpec((TM, LANES), tab), pl.BlockSpec((TM, LANES), tab),
                  pl.BlockSpec((TM, LANES), tab), pl.BlockSpec((TM, LANES), tab)],
        out_specs=[pl.BlockSpec((A_HEADS, A_QK, TM), lambda i: (0, 0, i)),
                   pl.BlockSpec((TM, A_QK), row),
                   pl.BlockSpec((TM // VB, DV, VB), lambda i: (i, 0, 0)),
                   pl.BlockSpec((B_HEADS, LANES, TM), lambda i: (0, 0, i)),
                   pl.BlockSpec((B_HEADS, TM, LANES), lambda i: (0, i, 0)),
                   pl.BlockSpec((B_HEADS, TM // VB, DV, VB), lambda i: (0, i, 0, 0))],
        out_shape=[jax.ShapeDtypeStruct((A_HEADS, A_QK, t_all), BF16),
                   jax.ShapeDtypeStruct((t_all, A_QK), BF16),
                   jax.ShapeDtypeStruct((t_all // VB, DV, VB), BF16),
                   jax.ShapeDtypeStruct((B_HEADS, LANES, t_all), BF16),
                   jax.ShapeDtypeStruct((B_HEADS, t_all, LANES), BF16),
                   jax.ShapeDtypeStruct((B_HEADS, t_all // VB, DV, VB), BF16)],
        compiler_params=_cparams(1),
        name="even_proj",
    )(h, mod3, win, gq, gkv, wuq, wabs, *tabs)


ITEMS = 1
N_SLOTS = 8
KEY_GROUP = 2


class _Flash:
    def __init__(self, s_refs, p_refs, acc_ref, runtime_offsets=False):
        self.s, self.p, self.acc = s_refs, p_refs, acc_ref
        self.dz = jnp.minimum(pl.program_id(0), 0) if runtime_offsets else None

    def rows(self, start, size):
        if self.dz is None:
            return slice(start, start + size)
        return pl.ds(pl.multiple_of(self.dz + start, BF16_ROWS), size)

    def qk(self, b, k, q_t, bias=None):
        rows = k.shape[0]
        s = _dot(k, q_t)
        if bias is not None:
            s = s + bias
        self.s[b % N_SLOTS][self.rows(0, rows), :] = s
        return jnp.max(s.reshape(rows // SUBLANES, SUBLANES, NQ), axis=0)

    def sm(self, blocks, rows, maxima, m_old):
        m8 = maxima[0]
        for x in maxima[1:]:
            m8 = jnp.maximum(m8, x)
        m_new = jnp.maximum(m_old, jnp.max(m8, axis=0, keepdims=True))
        alpha = jnp.exp2(m_old - m_new)
        for b in blocks:
            s_ref, p_ref = self.s[b % N_SLOTS], self.p[b % N_SLOTS]
            for r in range(rows // STRIP):
                rr = self.rows(STRIP * r, STRIP)
                p_ref[rr, :] = jnp.exp2(s_ref[rr, :] - m_new).astype(BF16)
        return m_new, alpha

    def pv(self, blocks, vt_lists, alpha):
        pv = None
        for b, vts in zip(blocks, vt_lists):
            for i, vt in enumerate(vts):
                d = _dot(vt, self.p[b % N_SLOTS][self.rows(VB * i, VB), :])
                pv = d if pv is None else pv + d
        self.acc[...] = pv if alpha is None else alpha * self.acc[...] + pv

    def run(self, q_t, key_blocks, group, first=0):
        groups, i = [], 0
        while i < len(key_blocks):
            j = i + 1
            while j < len(key_blocks) and j - i < group and key_blocks[j][0].shape == key_blocks[i][0].shape:
                j += 1
            groups.append(list(range(i, j)))
            i = j
        m = jnp.full((1, NQ), NEG_INF, F32)
        maxima, alphas = {}, {}
        for h in range(len(groups) + 2):
            if h < len(groups):
                for b in groups[h]:
                    blk = key_blocks[b]
                    maxima[b] = self.qk(first + b, blk[0], q_t, blk[2] if len(blk) > 2 else None)
            if 0 <= h - 1 < len(groups):
                g = groups[h - 1]
                m, alphas[h - 1] = self.sm([first + b for b in g], key_blocks[g[0]][0].shape[0],
                                           [maxima[b] for b in g], m)
            if 0 <= h - 2 < len(groups):
                g = groups[h - 2]
                self.pv([first + b for b in g], [key_blocks[b][1] for b in g], alphas[h - 2] if h > 2 else None)

    def normalised(self):
        return self.acc[0:LANES, :] * (1.0 / self.acc[LANES:LANES + 1, :])


def _flash_scratch():
    per_item = ([pltpu.VMEM((TK, NQ), F32)] * N_SLOTS + [pltpu.VMEM((TK, NQ), BF16)] * N_SLOTS
                + [pltpu.VMEM((DV, NQ), F32)])
    return per_item * ITEMS


def _flash_from_scratch(refs):
    per = 2 * N_SLOTS + 1
    return [_Flash(r[0:N_SLOTS], r[N_SLOTS:2 * N_SLOTS], r[2 * N_SLOTS])
            for r in (refs[per * u:per * (u + 1)] for u in range(ITEMS))]


def _block_diag_queries_t(q_t):
    row = lax.broadcasted_iota(jnp.int32, q_t.shape, 0)
    zero = jnp.zeros_like(q_t)
    half = LANES // 2
    return jnp.concatenate([jnp.where(row < half, q_t, zero), jnp.where(row >= half, q_t, zero)], axis=1)


def _mla_kernel(qa_ref, kl_ref, kc_ref, ctl_ref, ctc_ref, wuvt_ref, o_ref, *scratch, n_lat_steps, n_qb_lat):
    fls = _flash_from_scratch(scratch)
    is_latent = pl.program_id(1) < n_qb_lat
    ctx_block = lambda: (kc_ref[...], [ctc_ref[0]])
    lat_block = lambda t: (kl_ref[TK * t:TK * (t + 1), :], [ctl_ref[(TK // VB) * t + i] for i in range(TK // VB)])

    def group(gg, carry):
        gs = [ITEMS * gg + u for u in range(ITEMS)]
        q_ts = [jnp.concatenate([qa_ref[2 * g], qa_ref[2 * g + 1]], axis=1) for g in gs]

        @pl.when(is_latent)
        def _():
            for fl, q_t in zip(fls, q_ts):
                fl.run(q_t, [lat_block(t) for t in range(n_lat_steps)] + [ctx_block()], KEY_GROUP)

        @pl.when(jnp.logical_not(is_latent))
        def _():
            for fl, q_t in zip(fls, q_ts):
                fl.run(q_t, [ctx_block()], KEY_GROUP)

        for fl, g in zip(fls, gs):
            o_lat = fl.normalised().astype(BF16)
            for i in range(2):
                o = _dot(wuvt_ref[2 * g + i], o_lat[:, TQ * i:TQ * (i + 1)])
                o_ref[pl.ds(pl.multiple_of((2 * g + i) * A_V, A_V), A_V), :] = o.astype(BF16)
        return carry

    lax.fori_loop(0, A_HEADS // 2 // ITEMS, group, 0)


def _mla(qa, ka, ct, wuvt, bsz, n, m):
    t_lat = bsz * n
    t_all = t_lat + bsz * m
    nqb = n // TQ
    qblk = lambda b, j: jnp.where(j < nqb, b * nqb + j, t_lat // TQ + b)
    kern = functools.partial(_mla_kernel, n_lat_steps=n // TK, n_qb_lat=nqb)
    return pl.pallas_call(
        kern,
        grid=(bsz, nqb + 1),
        in_specs=[pl.BlockSpec((A_HEADS, A_QK, TQ), lambda b, j: (0, 0, qblk(b, j))),
                  _resident((n, A_QK), lambda b, j: (b, 0)),
                  pl.BlockSpec((m, A_QK), lambda b, j: (t_lat // m + b, 0)),
                  _resident((n // VB, DV, VB), lambda b, j: (b, 0, 0)),
                  pl.BlockSpec((m // VB, DV, VB), lambda b, j: (t_lat // m + b, 0, 0)),
                  pl.BlockSpec((A_HEADS, A_V, A_KV_RANK), lambda b, j: (0, 0, 0))],
        out_specs=pl.BlockSpec((A_HEADS * A_V, TQ), lambda b, j: (0, qblk(b, j))),
        out_shape=jax.ShapeDtypeStruct((A_HEADS * A_V, t_all), BF16),
        scratch_shapes=_flash_scratch(),
        compiler_params=_cparams(2),
        name="mla_attention",
    )(qa, ka, ka, ct, ct, wuvt)


def _diff_kernel(qb_ref, kl_ref, kc_ref, vl_ref, vc_ref, lam_ref, gsub_ref, o_ref, *scratch, n_lat_steps, n_qb_lat,
                 lam_init):
    fls = _flash_from_scratch(scratch)
    is_latent = pl.program_id(1) < n_qb_lat
    lv = lam_ref[...]
    lam = (jnp.exp(jnp.sum(lv[0:1] * lv[1:2], axis=1, keepdims=True))
           - jnp.exp(jnp.sum(lv[2:3] * lv[3:4], axis=1, keepdims=True)) + lam_init)

    def group(gg, carry):
        hds = [ITEMS * gg + u for u in range(ITEMS)]
        q_ts = [_block_diag_queries_t(qb_ref[hd]) for hd in hds]
        ctx_block = lambda hd: (kc_ref[hd], [vc_ref[hd, 0]])
        lat_block = lambda hd, t: (kl_ref[hd, TK * t:TK * (t + 1), :],
                                   [vl_ref[hd, (TK // VB) * t + i] for i in range(TK // VB)])

        @pl.when(is_latent)
        def _():
            for fl, q_t, hd in zip(fls, q_ts, hds):
                fl.run(q_t, [lat_block(hd, t) for t in range(n_lat_steps)] + [ctx_block(hd)], KEY_GROUP)

        @pl.when(jnp.logical_not(is_latent))
        def _():
            for fl, q_t, hd in zip(fls, q_ts, hds):
                fl.run(q_t, [ctx_block(hd)], KEY_GROUP)

        for fl, hd in zip(fls, hds):
            on = fl.normalised()
            o = on[:, 0:TQ] - lam * on[:, TQ:2 * TQ]
            o = o * lax.rsqrt(jnp.mean(o * o, axis=0, keepdims=True) + SUBLN_EPS) * gsub_ref[...]
            o_ref[pl.ds(pl.multiple_of(hd * LANES, LANES), LANES), :] = (o * (1.0 - lam_init)).astype(BF16)
        return carry

    lax.fori_loop(0, B_HEADS // ITEMS, group, 0)


def _diff(qb, kb, vbt, lam_vec, gsub_col, lam_init, bsz, n, m):
    t_lat = bsz * n
    t_all = t_lat + bsz * m
    nqb = n // TQ
    qblk = lambda b, j: jnp.where(j < nqb, b * nqb + j, t_lat // TQ + b)
    kern = functools.partial(_diff_kernel, n_lat_steps=n // TK, n_qb_lat=nqb, lam_init=lam_init)
    return pl.pallas_call(
        kern,
        grid=(bsz, nqb + 1),
        in_specs=[pl.BlockSpec((B_HEADS, LANES, TQ), lambda b, j: (0, 0, qblk(b, j))),
                  _resident((B_HEADS, n, LANES), lambda b, j: (0, b, 0)),
                  pl.BlockSpec((B_HEADS, m, LANES), lambda b, j: (0, t_lat // m + b, 0)),
                  _resident((B_HEADS, n // VB, DV, VB), lambda b, j: (0, b, 0, 0)),
                  pl.BlockSpec((B_HEADS, m // VB, DV, VB), lambda b, j: (0, t_lat // m + b, 0, 0)),
                  pl.BlockSpec((4, B_HD), lambda b, j: (0, 0)),
                  pl.BlockSpec((2 * B_HD, 1), lambda b, j: (0, 0))],
        out_specs=pl.BlockSpec((B_HEADS * LANES, TQ), lambda b, j: (0, qblk(b, j))),
        out_shape=jax.ShapeDtypeStruct((B_HEADS * LANES, t_all), BF16),
        scratch_shapes=_flash_scratch(),
        compiler_params=_cparams(2),
        name="diff_attention",
    )(qb, kb, kb, vbt, vbt, lam_vec, gsub_col)


def _odd_proj_kernel(h_ref, mod_ref, win_ref, u_ref, q_ref, k_ref, vt_ref):
    h = h_ref[...]
    hm = (h * (1.0 + mod_ref[0, 1:2, :]) + mod_ref[0, 0:1, :]).astype(BF16)
    p = _dot(hm, win_ref[...])
    u_ref[...] = p[:, 0:C_WIDTH]
    ones_rows = _ones_rows(TM)
    for s in range(D_WIDTH // LANES):
        sl = slice(LANES * s, LANES * (s + 1))
        q_ref[s] = (p[:, C_WIDTH:C_WIDTH + D_WIDTH][:, sl].T * D_QSCALE).astype(BF16)
        k_ref[s] = p[:, C_WIDTH + D_WIDTH:C_WIDTH + 2 * D_WIDTH][:, sl].astype(BF16)
        vt = p[:, C_WIDTH + 2 * D_WIDTH:C_WIDTH + 3 * D_WIDTH][:, sl].T.astype(BF16)
        vt = jnp.concatenate([vt, ones_rows], axis=0)
        for i in range(TM // VB):
            vt_ref[s, i] = vt[:, VB * i:VB * (i + 1)]


def _odd_proj(h, mod3, win, n_tiles, mod_index):
    t_all = n_tiles * TM
    slabs = D_WIDTH // LANES
    return pl.pallas_call(
        _odd_proj_kernel,
        grid=(n_tiles,),
        in_specs=[pl.BlockSpec((TM, D_MODEL), lambda i: (i, 0)),
                  pl.BlockSpec((1, 3, D_MODEL), lambda i: (mod_index(i), 0, 0)),
                  _resident((D_MODEL, C_WIDTH + 3 * D_WIDTH), lambda i: (0, 0))],
        out_specs=[pl.BlockSpec((TM, C_WIDTH), lambda i: (i, 0)),
                   pl.BlockSpec((slabs, LANES, TM), lambda i: (0, 0, i)),
                   pl.BlockSpec((slabs, TM, LANES), lambda i: (0, i, 0)),
                   pl.BlockSpec((slabs, TM // VB, DV, VB), lambda i: (0, i, 0, 0))],
        out_shape=[jax.ShapeDtypeStruct((t_all, C_WIDTH), F32),
                   jax.ShapeDtypeStruct((slabs, LANES, t_all), BF16),
                   jax.ShapeDtypeStruct((slabs, t_all, LANES), BF16),
                   jax.ShapeDtypeStruct((slabs, t_all // VB, DV, VB), BF16)],
        compiler_params=_cparams(1),
        name="odd_proj",
    )(h, mod3, win)


def _pool_kernel(prev_ref, u_ref, next_ref, w_ref, ps_ref, o_ref, ext_ref, *, tiles_per_seq):
    t = pl.program_id(0) % tiles_per_seq
    n = tiles_per_seq * TM
    ext_ref[0:POOL_HALO, :] = jnp.where(t > 0, prev_ref[...], 0.0)
    ext_ref[POOL_HALO:POOL_HALO + TM, :] = u_ref[...]
    ext_ref[POOL_HALO + TM:2 * POOL_HALO + TM, :] = jnp.where(t < tiles_per_seq - 1, next_ref[...], 0.0)
    pos = t * TM + lax.broadcasted_iota(jnp.int32, (TM, 1), 0)
    for g, w in enumerate(C_WINDOWS):
        left = w // 2
        right = w - 1 - left
        lanes = slice(C_GW * g, C_GW * (g + 1))
        tot = ext_ref[POOL_HALO - left:POOL_HALO - left + TM, lanes]
        for k in range(1 - left, right + 1):
            tot = tot + ext_ref[POOL_HALO + k:POOL_HALO + k + TM, lanes]
        cnt = jnp.minimum(pos + right + 1, n) - jnp.maximum(pos - left, 0)
        pooled = tot / cnt.astype(F32) - u_ref[:, lanes]
        y = _dot(pooled.astype(BF16), w_ref[g])
        o_ref[:, lanes] = (y * ps_ref[:, lanes]).astype(BF16)


def _pool(u, w_pool, pool_scale, n_tiles, tiles_per_seq):
    per = TM // POOL_HALO
    last = u.shape[0] // POOL_HALO - 1
    return pl.pallas_call(
        functools.partial(_pool_kernel, tiles_per_seq=tiles_per_seq),
        grid=(n_tiles,),
        in_specs=[pl.BlockSpec((POOL_HALO, C_WIDTH), lambda i: (jnp.maximum(i * per - 1, 0), 0)),
                  pl.BlockSpec((TM, C_WIDTH), lambda i: (i, 0)),
                  pl.BlockSpec((POOL_HALO, C_WIDTH), lambda i: (jnp.minimum((i + 1) * per, last), 0)),
                  pl.BlockSpec((C_GROUPS, C_GW, C_GW), lambda i: (0, 0, 0)),
                  pl.BlockSpec((1, C_WIDTH), lambda i: (0, 0))],
        out_specs=pl.BlockSpec((TM, C_WIDTH), lambda i: (i, 0)),
        out_shape=jax.ShapeDtypeStruct((n_tiles * TM, C_WIDTH), BF16),
        scratch_shapes=[pltpu.VMEM((TM + 2 * POOL_HALO, C_WIDTH), F32)],
        compiler_params=_cparams(1),
        name="multiscale_pool",
    )(u, u, u, w_pool, pool_scale)


def _na_row_table(rows):
    qr_per = TQ // GRID_W
    nblk = rows // qr_per
    g_of = (0, 1, nblk - 1)
    gb_of = tuple(min(max(g - 1, 0), nblk - 3) for g in g_of)
    table = {}
    for p in range(NA_PATTERNS):
        for kr in range(NA_KEY_ROWS):
            for qr in range(qr_per):
                r = qr_per * g_of[p] + qr
                ka = qr_per * gb_of[p] + kr
                r0 = min(max(r - NA_ROWS // 2, 0), rows - NA_ROWS)
                table[p, kr, qr] = (r0 <= ka < r0 + NA_ROWS, ka - r + NA_ROWS - 1)
    return table


def _na_bias_kernel(rpb_ref, o_ref, *, rows):
    hd = pl.program_id(0)
    n_r, n_c = 2 * NA_ROWS - 1, 2 * NA_COLS - 1
    qr_per = TQ // GRID_W
    kc = lax.broadcasted_iota(jnp.int32, (GRID_W, TQ), 0)
    q = lax.broadcasted_iota(jnp.int32, (GRID_W, TQ), 1)
    qc, qrow = q % GRID_W, q // GRID_W
    idx = kc - qc + NA_COLS - 1
    cs = jnp.clip(qc - NA_COLS // 2, 0, GRID_W - NA_COLS)
    col_ok = (kc >= cs) & (kc < cs + NA_COLS)
    neg = jnp.full((GRID_W, TQ), NEG_INF, F32)
    toeplitz = []
    for dr in range(n_r):
        acc = neg
        for j in range(n_c):
            acc = jnp.where(idx == j, rpb_ref[hd * n_r * n_c + dr * n_c + j], acc)
        toeplitz.append(jnp.where(col_ok, acc * LOG2E, neg))
    table = _na_row_table(rows)
    for p in range(NA_PATTERNS):
        for kr in range(NA_KEY_ROWS):
            blk = neg
            for qr in range(qr_per):
                ok, ridx = table[p, kr, qr]
                if ok:
                    blk = jnp.where(qrow == qr, toeplitz[ridx], blk)
            o_ref[p, 0, GRID_W * kr:GRID_W * (kr + 1), :] = blk


def _na_bias(rpb, rows):
    n_r, n_c = 2 * NA_ROWS - 1, 2 * NA_COLS - 1
    n_keys = NA_KEY_ROWS * GRID_W
    return pl.pallas_call(
        functools.partial(_na_bias_kernel, rows=rows),
        grid=(D_HEADS,),
        in_specs=[pl.BlockSpec(memory_space=pltpu.SMEM)],
        out_specs=pl.BlockSpec((NA_PATTERNS, 1, n_keys, TQ), lambda i: (0, i // 2, 0, i % 2)),
        out_shape=jax.ShapeDtypeStruct((NA_PATTERNS, D_HEADS // 2, n_keys, 2 * TQ), F32),
        compiler_params=_cparams(1),
        name="na_bias",
    )(rpb.reshape(D_HEADS * n_r * n_c))


def _na_kernel(q_ref, kl_ref, kc_ref, vl_ref, vc_ref, bias_ref, o_ref, *scratch, nqb):
    s_refs, p_refs, accs = scratch[0:N_SLOTS], scratch[N_SLOTS:2 * N_SLOTS], scratch[2 * N_SLOTS:]
    n_blk = NA_KEY_ROWS * GRID_W // VB
    half = LANES // 2
    for i in range(NA_GS):
        fl = _Flash(s_refs, p_refs, accs[i % len(accs)], runtime_offsets=True)
        g = pl.program_id(2) * NA_GS + i
        gb = jnp.clip(g - 1, 0, nqb - n_blk)
        pat = jnp.where(g == 0, 0, jnp.where(g == nqb - 1, 2, 1))
        cols = slice(TQ * i, TQ * (i + 1))
        q_t = _block_diag_queries_t(q_ref[0, :, cols])
        blocks = [(kl_ref[0, pl.ds(pl.multiple_of((gb + j) * VB, VB), VB), :], [vl_ref[0, gb + j]],
                   bias_ref[pat, 0, VB * j:VB * (j + 1), :]) for j in range(n_blk)]
        blocks.append((kc_ref[0], [vc_ref[0, 0]]))
        fl.run(q_t, blocks, len(blocks), first=len(blocks) * i)
        on = fl.normalised()
        o_ref[0:half, cols] = on[0:half, 0:TQ].astype(BF16)
        o_ref[half:LANES, cols] = on[half:LANES, TQ:2 * TQ].astype(BF16)


def _natten(qn, kn, vnt, bias, bsz, n, m):
    t_lat = bsz * n
    nqb = n // TQ
    steps = nqb // NA_GS
    slabs = D_WIDTH // LANES
    return pl.pallas_call(
        functools.partial(_na_kernel, nqb=nqb),
        grid=(bsz, slabs, steps),
        in_specs=[pl.BlockSpec((1, LANES, NA_GS * TQ), lambda b, s, g: (s, 0, b * steps + g)),
                  pl.BlockSpec((1, n, LANES), lambda b, s, g: (s, b, 0)),
                  pl.BlockSpec((1, m, LANES), lambda b, s, g: (s, t_lat // m + b, 0)),
                  pl.BlockSpec((1, n // VB, DV, VB), lambda b, s, g: (s, b, 0, 0)),
                  pl.BlockSpec((1, m // VB, DV, VB), lambda b, s, g: (s, t_lat // m + b, 0, 0)),
                  pl.BlockSpec((NA_PATTERNS, 1, NA_KEY_ROWS * GRID_W, 2 * TQ), lambda b, s, g: (0, s, 0, 0))],
        out_specs=pl.BlockSpec((LANES, NA_GS * TQ), lambda b, s, g: (s, b * steps + g)),
        out_shape=jax.ShapeDtypeStruct((D_WIDTH, t_lat), BF16),
        scratch_shapes=([pltpu.VMEM((VB, NQ), F32)] * N_SLOTS + [pltpu.VMEM((VB, NQ), BF16)] * N_SLOTS
                        + [pltpu.VMEM((DV, NQ), F32)] * 2),
        compiler_params=_cparams(3),
        name="neighbourhood_attention",
    )(qn, kn, kn, vnt, vnt, bias)


def _rope_tables(n, rot_dim):
    t = np.arange(n)
    row = (t // GRID_W).astype(np.float32)
    col = (t % GRID_W).astype(np.float32)
    quarter = rot_dim // 4
    inv = jnp.asarray(ROPE_BASE, F32) ** (-jnp.arange(quarter, dtype=F32) / quarter)
    ang_r = jnp.asarray(row)[:, None] * inv[None, :]
    ang_c = jnp.asarray(col)[:, None] * inv[None, :]
    ang = jnp.concatenate([ang_r, ang_r, ang_c, ang_c], -1)
    sign = np.where((np.arange(rot_dim) % (2 * quarter)) < quarter, -1.0, 1.0).astype(np.float32)
    cos = jnp.tile(jnp.cos(ang), (1, LANES // rot_dim))
    sin = jnp.tile(jnp.sin(ang) * sign[None, :], (1, LANES // rot_dim))
    cos = jnp.concatenate([cos, jnp.ones((TM, LANES), F32)], 0)
    sin = jnp.concatenate([sin, jnp.zeros((TM, LANES), F32)], 0)
    return cos, sin


def _even_weights(w_in, w_uq, w_ukv):
    o = A_Q_RANK + B_QK
    kpe = jnp.pad(w_in[:, o + A_KV_RANK:o + A_KV_RANK + A_ROPE], ((0, 0), (0, LANES - A_ROPE)))
    win = jnp.concatenate([w_in[:, 0:A_Q_RANK], w_in[:, A_Q_RANK:o], w_in[:, o:o + A_KV_RANK], kpe,
                           w_in[:, o + A_KV_RANK + A_ROPE:o + A_KV_RANK + A_ROPE + B_QK],
                           w_in[:, o + A_KV_RANK + A_ROPE + B_QK:]], axis=1).astype(BF16)
    uq = w_uq.reshape(A_Q_RANK, A_HEADS, A_NOPE + A_ROPE)
    wuq = jnp.concatenate([uq[:, :, :A_NOPE].reshape(A_Q_RANK, A_HEADS * A_NOPE),
                           uq[:, :, A_NOPE:].reshape(A_Q_RANK, A_HEADS * A_ROPE)], axis=1).astype(BF16)
    ukv = w_ukv.reshape(A_KV_RANK, A_HEADS, A_NOPE + A_V)
    uk_t = jnp.transpose(ukv[:, :, :A_NOPE], (1, 2, 0))
    wabs = jnp.zeros((A_HEADS, A_NOPE, A_HEADS, A_KV_RANK), F32)
    wabs = wabs.at[np.arange(A_HEADS), :, np.arange(A_HEADS), :].set(uk_t)
    wabs = wabs.reshape(A_HEADS * A_NOPE, A_HEADS * A_KV_RANK).astype(BF16)
    wuvt = jnp.transpose(ukv[:, :, A_NOPE:], (1, 2, 0)).astype(BF16)
    return win, wuq, wabs, wuvt


def kernel(x, c, ctx, c_ctx, ada_w, ada_b, ln_g, ln_b, ffn_w_gate, ffn_w_up, ffn_w_down, ev_w_in, ev_w_out,
           ev_g_qlat, ev_g_kvlat, ev_w_uq, ev_w_ukv, ev_lam, ev_g_sub, od_w_in, od_w_out, od_w_pool,
           od_pool_scale, od_rpb):
    bsz, n, d = x.shape
    m = ctx.shape[1]
    assert d == D_MODEL and ada_w.shape[0] == DEPTH == 2
    assert m == TQ and n % FFN_ROWS == 0 and n % GRID_W == 0 and (bsz * m) % FFN_ROWS == 0 and bsz + 1 <= 8
    assert n // TQ >= 3 and (n // TQ) % NA_GS == 0
    t_lat = bsz * n
    t_all = t_lat + bsz * m
    tiles_lat, tiles_all, tiles_seq = t_lat // TM, t_all // TM, n // TM
    mod_index = lambda i: jnp.minimum(i // tiles_seq, bsz)
    tab_index = lambda i: jnp.where(i < tiles_lat, i % tiles_seq, tiles_seq)

    c_all = jnp.zeros((8, D_MODEL), F32).at[:bsz].set(c).at[bsz].set(c_ctx)
    mod = _modulation(c_all, ada_w, ada_b).reshape(DEPTH, 8, N_MOD, D_MODEL)
    row = lambda v: v.reshape(1, -1)
    wg, wu, wd = ffn_w_gate.astype(BF16), ffn_w_up.astype(BF16), ffn_w_down.astype(BF16)

    h = _ffn(x.reshape(t_lat, D_MODEL), mod[0, :, 0:3], wg[0, 0], wu[0, 0], wd[0, 0], row(ln_g[0, 0]), row(ln_b[0, 0]),
             t_all, n, bsz, h_tail=ctx.reshape(bsz * m, D_MODEL))
    win, wuq, wabs, wuvt = _even_weights(ev_w_in[0], ev_w_uq[0], ev_w_ukv[0])
    tabs = _rope_tables(n, A_ROPE) + _rope_tables(n, B_HD)
    qa, ka, ct, qb, kb, vbt = _even_proj(h, mod[0, :, 3:6], win, row(ev_g_qlat[0]), row(ev_g_kvlat[0]), wuq, wabs,
                                         tabs, tiles_all, mod_index, tab_index)
    lam_init = 0.8 - 0.6 * math.exp(-0.3 * 0)
    mix_a = _mla(qa, ka, ct, wuvt, bsz, n, m)
    mix_b = _diff(qb, kb, vbt, ev_lam[0], ev_g_sub[0].reshape(2 * B_HD, 1), lam_init, bsz, n, m)
    h = _ffn(h, mod[0, :, 3:9], wg[0, 1], wu[0, 1], wd[0, 1], row(ln_g[0, 2]), row(ln_b[0, 2]), t_all, n, bsz,
             mixer=("tt", mix_a, mix_b, ev_w_out[0].astype(BF16), row(ln_g[0, 1]), row(ln_b[0, 1])))

    h = _ffn(h, mod[1, :, 0:3], wg[1, 0], wu[1, 0], wd[1, 0], row(ln_g[1, 0]), row(ln_b[1, 0]), t_all, n, bsz)
    u, qn, kn, vnt = _odd_proj(h, mod[1, :, 3:6], od_w_in[0].astype(BF16), tiles_all, mod_index)
    y_pool = _pool(u, od_w_pool[0].astype(BF16), row(od_pool_scale[0]), tiles_lat, tiles_seq)
    bias = _na_bias(od_rpb[0], n // GRID_W)
    y_na = _natten(qn, kn, vnt, bias, bsz, n, m)
    h = _ffn(h, mod[1, :, 3:9], wg[1, 1], wu[1, 1], wd[1, 1], row(ln_g[1, 2]), row(ln_b[1, 2]), t_lat, n, bsz,
             mixer=("nt", y_pool, y_na, od_w_out[0].astype(BF16), row(ln_g[1, 1]), row(ln_b[1, 1])))
    return h.reshape(bsz, n, D_MODEL)
```

```python
import functools
import math

import numpy as np
import jax
import jax.numpy as jnp
from jax import lax
from jax.experimental import pallas as pl
from jax.experimental.pallas import tpu as pltpu

F32 = jnp.float32
BF16 = jnp.bfloat16

D_MODEL = 1024
DEPTH = 2
GRID_W = 64
N_MOD = 9
DN_ALPHA = float((2 * DEPTH) ** 0.25)
LN_EPS = 1e-6
RMS_EPS = 1e-6
SUBLN_EPS = 1e-5
ROPE_BASE = 10000.0
NEG_INF = -1e30
D_FF = 2816

A_HEADS = 8
A_NOPE = 64
A_ROPE = 32
A_V = 64
A_Q_RANK = 256
A_KV_RANK = 128
A_SCALE = float((A_NOPE + A_ROPE) ** -0.5)
A_QK = A_KV_RANK + A_ROPE

B_HEADS = 4
B_HD = 64
B_SCALE = float(B_HD ** -0.5)
B_QK = B_HEADS * 2 * B_HD

C_GROUPS = 4
C_WINDOWS = (2, 4, 8, 16)
C_GW = 128
C_WIDTH = C_GROUPS * C_GW

D_HEADS = 8
D_HD = 64
D_SCALE = float(D_HD ** -0.5)
D_WIDTH = D_HEADS * D_HD
NA_ROWS = 8
NA_COLS = 16

LOG2E = math.log2(math.e)
A_QSCALE = A_SCALE * LOG2E
B_QSCALE = B_SCALE * LOG2E
D_QSCALE = D_SCALE * LOG2E

LANES = 128
SUBLANES = 8
TM = 512
FFN_ROWS = 2 * TM
TQ = 256
NQ = 2 * TQ
TK = 512
VB = 256
STRIP = 64
BF16_ROWS = 16
DV = LANES + BF16_ROWS
MXU_DIM = 256
FF_SPLITS = (0, 6 * MXU_DIM, D_FF)
POOL_HALO = 8
NA_KEY_ROWS = 12
NA_PATTERNS = 3
NA_GS = 8
VMEM_LIMIT = 56 * 1024 * 1024


def _cparams(n_axes):
    return pltpu.CompilerParams(dimension_semantics=("arbitrary",) * n_axes,
                                vmem_limit_bytes=VMEM_LIMIT)


def _resident(shape, index_map):
    return pl.BlockSpec(shape, index_map, pipeline_mode=pl.Buffered(1))


def _layer_norm(z, g, b):
    mu = jnp.mean(z, axis=-1, keepdims=True)
    zc = z - mu
    var = jnp.mean(zc * zc, axis=-1, keepdims=True)
    return zc * lax.rsqrt(var + LN_EPS) * g + b


def _rms_norm(x, g, eps):
    return x * lax.rsqrt(jnp.mean(x * x, axis=-1, keepdims=True) + eps) * g


def _dot(a, b):
    return jnp.dot(a, b, preferred_element_type=F32)


def _dot_tn(a, b):
    return lax.dot_general(a, b, (((0,), (0,)), ((), ())), preferred_element_type=F32)


def _mod_kernel(c_ref, w_ref, b_ref, o_ref):
    x = c_ref[...]
    s = (x * jax.nn.sigmoid(x)).astype(BF16)
    o_ref[0] = _dot(s, w_ref[0].astype(BF16)) + b_ref[0]


def _modulation(c_all, ada_w, ada_b):
    depth = ada_w.shape[0]
    return pl.pallas_call(
        _mod_kernel,
        grid=(depth, N_MOD),
        in_specs=[pl.BlockSpec((8, D_MODEL), lambda l, j: (0, 0)),
                  pl.BlockSpec((1, D_MODEL, D_MODEL), lambda l, j: (l, 0, j)),
                  pl.BlockSpec((1, 1, D_MODEL), lambda l, j: (l, 0, j))],
        out_specs=pl.BlockSpec((1, 8, D_MODEL), lambda l, j: (l, 0, j)),
        out_shape=jax.ShapeDtypeStruct((depth, 8, N_MOD * D_MODEL), F32),
        compiler_params=_cparams(2),
        name="modulation",
    )(c_all, ada_w, ada_b.reshape(depth, 1, N_MOD * D_MODEL))


def _ffn_kernel(*refs, n_first, mix):
    n_src = len(refs) - 7 - (5 if mix else 0)
    h_refs, (mod_ref, wg_ref, wu_ref, wd_ref, g_ref, b_ref, o_ref) = refs[:n_src], refs[-7:]
    off = 3 if mix else 0
    shift, scale, gate = mod_ref[0, off:off + 1, :], mod_ref[0, off + 1:off + 2, :], mod_ref[0, off + 2:off + 3, :]
    for part in range(FFN_ROWS // TM):
        rows = slice(TM * part, TM * (part + 1))
        h = h_refs[0][rows, :]
        if n_src == 2:
            h = jnp.where(pl.program_id(0) < n_first, h, h_refs[1][rows, :])
        if mix:
            a_ref, bm_ref, wo_ref, g1_ref, b1_ref = refs[n_src:n_src + 5]
            half = wo_ref.shape[0] // 2
            ya = (_dot_tn(a_ref[:, rows], wo_ref[0:half, :]) if mix[0] == "t"
                  else _dot(a_ref[rows, :], wo_ref[0:half, :]))
            y_mix = ya + _dot_tn(bm_ref[:, rows], wo_ref[half:2 * half, :])
            h = _layer_norm(DN_ALPHA * h + mod_ref[0, 2:3, :] * y_mix, g1_ref[...], b1_ref[...])
        hm = (h * (1.0 + scale) + shift).astype(BF16)
        y = jnp.zeros((TM, D_MODEL), F32)
        for c in range(len(FF_SPLITS) - 1):
            cols = slice(FF_SPLITS[c], FF_SPLITS[c + 1])
            gt = _dot(hm, wg_ref[:, cols])
            up = _dot(hm, wu_ref[:, cols])
            act = (gt * jax.nn.sigmoid(gt) * up).astype(BF16)
            y = y + _dot(act, wd_ref[cols, :])
        o_ref[rows, :] = _layer_norm(DN_ALPHA * h + (0.5 * gate) * y, g_ref[...], b_ref[...])


def _ffn(h, mods, wg, wu, wd, g, b, n_rows, seq_len, bsz, h_tail=None, mixer=None):
    n_first = h.shape[0] // FFN_ROWS
    mod_index = lambda i: jnp.minimum(i // (seq_len // FFN_ROWS), bsz)
    if h_tail is None:
        sources, src_specs = [h], [pl.BlockSpec((FFN_ROWS, D_MODEL), lambda i: (i, 0))]
    else:
        sources = [h, h_tail]
        src_specs = [pl.BlockSpec((FFN_ROWS, D_MODEL), lambda i: (jnp.minimum(i, n_first - 1), 0)),
                     pl.BlockSpec((FFN_ROWS, D_MODEL), lambda i: (jnp.maximum(i - n_first, 0), 0))]
    layouts = None
    if mixer is not None:
        layouts, mix_a, mix_b, w_out, g1, b1 = mixer
        half = w_out.shape[0] // 2
        t_spec = pl.BlockSpec((half, FFN_ROWS), lambda i: (0, i))
        n_spec = pl.BlockSpec((FFN_ROWS, half), lambda i: (i, 0))
        sources += [mix_a, mix_b, w_out, g1, b1]
        src_specs += [t_spec if layouts[0] == "t" else n_spec, t_spec,
                      _resident((2 * half, D_MODEL), lambda i: (0, 0)),
                      pl.BlockSpec((1, D_MODEL), lambda i: (0, 0)), pl.BlockSpec((1, D_MODEL), lambda i: (0, 0))]
    return pl.pallas_call(
        functools.partial(_ffn_kernel, n_first=n_first, mix=layouts),
        grid=(n_rows // FFN_ROWS,),
        in_specs=src_specs + [
                  pl.BlockSpec((1, mods.shape[1], D_MODEL), lambda i: (mod_index(i), 0, 0)),
                  _resident((D_MODEL, D_FF), lambda i: (0, 0)),
                  _resident((D_MODEL, D_FF), lambda i: (0, 0)),
                  _resident((D_FF, D_MODEL), lambda i: (0, 0)),
                  pl.BlockSpec((1, D_MODEL), lambda i: (0, 0)),
                  pl.BlockSpec((1, D_MODEL), lambda i: (0, 0))],
        out_specs=pl.BlockSpec((FFN_ROWS, D_MODEL), lambda i: (i, 0)),
        out_shape=jax.ShapeDtypeStruct((n_rows, D_MODEL), F32),
        compiler_params=_cparams(1),
        name="ffn_postnorm",
    )(*sources, mods, wg, wu, wd, g, b)


def _ones_rows(n):
    row = lax.broadcasted_iota(jnp.int32, (BF16_ROWS, n), 0)
    return jnp.where(row == 0, 1.0, 0.0).astype(BF16)


def _rope(x, cos, sin_signed, quarter):
    lane = lax.broadcasted_iota(jnp.int32, x.shape, 1)
    first = (lane % (2 * quarter)) < quarter
    rot = jnp.where(first, pltpu.roll(x, LANES - quarter, 1), pltpu.roll(x, quarter, 1))
    return x * cos + rot * sin_signed


EV_QLAT = 0
EV_BQ = 256
EV_KVLAT = 768
EV_KPE = 896
EV_BK = 1024
EV_BV = 1536
EV_PCOLS = 2048


def _even_proj_kernel(h_ref, mod_ref, win_ref, gq_ref, gkv_ref, wuq_ref, wabs_ref,
                      cosa_ref, sina_ref, cosb_ref, sinb_ref,
                      qa_ref, ka_ref, ct_ref, qb_ref, kb_ref, vbt_ref):
    h = h_ref[...]
    hm = (h * (1.0 + mod_ref[0, 1:2, :]) + mod_ref[0, 0:1, :]).astype(BF16)
    p = _dot(hm, win_ref[...])
    cosa, sina, cosb, sinb = cosa_ref[...], sina_ref[...], cosb_ref[...], sinb_ref[...]

    qn = _rms_norm(p[:, EV_QLAT:EV_QLAT + A_Q_RANK], gq_ref[...], RMS_EPS).astype(BF16)
    q = _dot(qn, wuq_ref[...])
    n_nope = A_HEADS * A_NOPE
    qabs = _dot(q[:, :n_nope].astype(BF16), wabs_ref[...])
    qpe = [_rope(q[:, n_nope + LANES * s:n_nope + LANES * (s + 1)], cosa, sina, A_ROPE // 4) for s in range(2)]
    qpe_t = jnp.concatenate(qpe, axis=1).T
    for hd in range(A_HEADS):
        qa_ref[hd, 0:A_KV_RANK, :] = (qabs[:, A_KV_RANK * hd:A_KV_RANK * (hd + 1)].T * A_QSCALE).astype(BF16)
        qa_ref[hd, A_KV_RANK:A_QK, :] = (qpe_t[A_ROPE * hd:A_ROPE * (hd + 1), :] * A_QSCALE).astype(BF16)

    cn = _rms_norm(p[:, EV_KVLAT:EV_KVLAT + A_KV_RANK], gkv_ref[...], RMS_EPS)
    kpe = _rope(p[:, EV_KPE:EV_KPE + LANES], cosa, sina, A_ROPE // 4)
    ka_ref[:, 0:A_KV_RANK] = cn.astype(BF16)
    ka_ref[:, A_KV_RANK:A_QK] = kpe[:, 0:A_ROPE].astype(BF16)
    ones_rows = _ones_rows(TM)
    cnt = jnp.concatenate([cn.T.astype(BF16), ones_rows], axis=0)
    for s in range(TM // VB):
        ct_ref[s] = cnt[:, VB * s:VB * (s + 1)]

    for hd in range(B_HEADS):
        sl = slice(LANES * hd, LANES * (hd + 1))
        qb_ref[hd] = (_rope(p[:, EV_BQ:EV_BQ + B_QK][:, sl], cosb, sinb, B_HD // 4).T * B_QSCALE).astype(BF16)
        kb_ref[hd] = _rope(p[:, EV_BK:EV_BK + B_QK][:, sl], cosb, sinb, B_HD // 4).astype(BF16)
        vt = jnp.concatenate([p[:, EV_BV:EV_BV + B_QK][:, sl].T.astype(BF16), ones_rows], axis=0)
        for s in range(TM // VB):
            vbt_ref[hd, s] = vt[:, VB * s:VB * (s + 1)]


def _even_proj(h, mod3, win, gq, gkv, wuq, wabs, tabs, n_tiles, mod_index, tab_index):
    t_all = n_tiles * TM
    row = lambda i: (i, 0)
    tab = lambda i: (tab_index(i), 0)
    const = lambda i: (0, 0)
    return pl.pallas_call(
        _even_proj_kernel,
        grid=(n_tiles,),
        in_specs=[pl.BlockSpec((TM, D_MODEL), row),
                  pl.BlockSpec((1, 3, D_MODEL), lambda i: (mod_index(i), 0, 0)),
                  _resident((D_MODEL, EV_PCOLS), const),
                  pl.BlockSpec((1, A_Q_RANK), const),
                  pl.BlockSpec((1, A_KV_RANK), const),
                  _resident((A_Q_RANK, A_HEADS * (A_NOPE + A_ROPE)), const),
                  _resident((A_HEADS * A_NOPE, A_HEADS * A_KV_RANK), const),
                  pl.BlockSpec((TM, LANES), tab), pl.BlockSpec((TM, LANES), tab),
                  pl.BlockSpec((TM, LANES), tab), pl.BlockSpec((TM, LANES), tab)],
        out_specs=[pl.BlockSpec((A_HEADS, A_QK, TM), lambda i: (0, 0, i)),
                   pl.BlockSpec((TM, A_QK), row),
                   pl.BlockSpec((TM // VB, DV, VB), lambda i: (i, 0, 0)),
                   pl.BlockSpec((B_HEADS, LANES, TM), lambda i: (0, 0, i)),
                   pl.BlockSpec((B_HEADS, TM, LANES), lambda i: (0, i, 0)),
                   pl.BlockSpec((B_HEADS, TM // VB, DV, VB), lambda i: (0, i, 0, 0))],
        out_shape=[jax.ShapeDtypeStruct((A_HEADS, A_QK, t_all), BF16),
                   jax.ShapeDtypeStruct((t_all, A_QK), BF16),
                   jax.ShapeDtypeStruct((t_all // VB, DV, VB), BF16),
                   jax.ShapeDtypeStruct((B_HEADS, LANES, t_all), BF16),
                   jax.ShapeDtypeStruct((B_HEADS, t_all, LANES), BF16),
                   jax.ShapeDtypeStruct((B_HEADS, t_all // VB, DV, VB), BF16)],
        compiler_params=_cparams(1),
        name="even_proj",
    )(h, mod3, win, gq, gkv, wuq, wabs, *tabs)


ITEMS = 1
N_SLOTS = 8
KEY_GROUP = 2


class _Flash:
    def __init__(self, s_refs, p_refs, acc_ref, runtime_offsets=False):
        self.s, self.p, self.acc = s_refs, p_refs, acc_ref
        self.dz = jnp.minimum(pl.program_id(0), 0) if runtime_offsets else None

    def rows(self, start, size):
        if self.dz is None:
            return slice(start, start + size)
        return pl.ds(pl.multiple_of(self.dz + start, BF16_ROWS), size)

    def qk(self, b, k, q_t, bias=None):
        rows = k.shape[0]
        s = _dot(k, q_t)
        if bias is not None:
            s = s + bias
        self.s[b % N_SLOTS][self.rows(0, rows), :] = s
        return jnp.max(s.reshape(rows // SUBLANES, SUBLANES, NQ), axis=0)

    def sm(self, blocks, rows, maxima, m_old):
        m8 = maxima[0]
        for x in maxima[1:]:
            m8 = jnp.maximum(m8, x)
        m_new = jnp.maximum(m_old, jnp.max(m8, axis=0, keepdims=True))
        alpha = jnp.exp2(m_old - m_new)
        for b in blocks:
            s_ref, p_ref = self.s[b % N_SLOTS], self.p[b % N_SLOTS]
            for r in range(rows // STRIP):
                rr = self.rows(STRIP * r, STRIP)
                p_ref[rr, :] = jnp.exp2(s_ref[rr, :] - m_new).astype(BF16)
        return m_new, alpha

    def pv(self, blocks, vt_lists, alpha):
        pv = None
        for b, vts in zip(blocks, vt_lists):
            for i, vt in enumerate(vts):
                d = _dot(vt, self.p[b % N_SLOTS][self.rows(VB * i, VB), :])
                pv = d if pv is None else pv + d
        self.acc[...] = pv if alpha is None else alpha * self.acc[...] + pv

    def run(self, q_t, key_blocks, group, first=0):
        groups, i = [], 0
        while i < len(key_blocks):
            j = i + 1
            while j < len(key_blocks) and j - i < group and key_blocks[j][0].shape == key_blocks[i][0].shape:
                j += 1
            groups.append(list(range(i, j)))
            i = j
        m = jnp.full((1, NQ), NEG_INF, F32)
        maxima, alphas = {}, {}
        for h in range(len(groups) + 2):
            if h < len(groups):
                for b in groups[h]:
                    blk = key_blocks[b]
                    maxima[b] = self.qk(first + b, blk[0], q_t, blk[2] if len(blk) > 2 else None)
            if 0 <= h - 1 < len(groups):
                g = groups[h - 1]
                m, alphas[h - 1] = self.sm([first + b for b in g], key_blocks[g[0]][0].shape[0],
                                           [maxima[b] for b in g], m)
            if 0 <= h - 2 < len(groups):
                g = groups[h - 2]
                self.pv([first + b for b in g], [key_blocks[b][1] for b in g], alphas[h - 2] if h > 2 else None)

    def normalised(self):
        return self.acc[0:LANES, :] * (1.0 / self.acc[LANES:LANES + 1, :])


def _flash_scratch():
    per_item = ([pltpu.VMEM((TK, NQ), F32)] * N_SLOTS + [pltpu.VMEM((TK, NQ), BF16)] * N_SLOTS
                + [pltpu.VMEM((DV, NQ), F32)])
    return per_item * ITEMS


def _flash_from_scratch(refs):
    per = 2 * N_SLOTS + 1
    return [_Flash(r[0:N_SLOTS], r[N_SLOTS:2 * N_SLOTS], r[2 * N_SLOTS])
            for r in (refs[per * u:per * (u + 1)] for u in range(ITEMS))]


def _block_diag_queries_t(q_t):
    row = lax.broadcasted_iota(jnp.int32, q_t.shape, 0)
    zero = jnp.zeros_like(q_t)
    half = LANES // 2
    return jnp.concatenate([jnp.where(row < half, q_t, zero), jnp.where(row >= half, q_t, zero)], axis=1)


def _mla_kernel(qa_ref, kl_ref, kc_ref, ctl_ref, ctc_ref, wuvt_ref, o_ref, *scratch, n_lat_steps, n_qb_lat):
    fls = _flash_from_scratch(scratch)
    is_latent = pl.program_id(1) < n_qb_lat
    ctx_block = lambda: (kc_ref[...], [ctc_ref[0]])
    lat_block = lambda t: (kl_ref[TK * t:TK * (t + 1), :], [ctl_ref[(TK // VB) * t + i] for i in range(TK // VB)])

    def group(gg, carry):
        gs = [ITEMS * gg + u for u in range(ITEMS)]
        q_ts = [jnp.concatenate([qa_ref[2 * g], qa_ref[2 * g + 1]], axis=1) for g in gs]

        @pl.when(is_latent)
        def _():
            for fl, q_t in zip(fls, q_ts):
                fl.run(q_t, [lat_block(t) for t in range(n_lat_steps)] + [ctx_block()], KEY_GROUP)

        @pl.when(jnp.logical_not(is_latent))
        def _():
            for fl, q_t in zip(fls, q_ts):
                fl.run(q_t, [ctx_block()], KEY_GROUP)

        for fl, g in zip(fls, gs):
            o_lat = fl.normalised().astype(BF16)
            for i in range(2):
                o = _dot(wuvt_ref[2 * g + i], o_lat[:, TQ * i:TQ * (i + 1)])
                o_ref[pl.ds(pl.multiple_of((2 * g + i) * A_V, A_V), A_V), :] = o.astype(BF16)
        return carry

    lax.fori_loop(0, A_HEADS // 2 // ITEMS, group, 0)


def _mla(qa, ka, ct, wuvt, bsz, n, m):
    t_lat = bsz * n
    t_all = t_lat + bsz * m
    nqb = n // TQ
    qblk = lambda b, j: jnp.where(j < nqb, b * nqb + j, t_lat // TQ + b)
    kern = functools.partial(_mla_kernel, n_lat_steps=n // TK, n_qb_lat=nqb)
    return pl.pallas_call(
        kern,
        grid=(bsz, nqb + 1),
        in_specs=[pl.BlockSpec((A_HEADS, A_QK, TQ), lambda b, j: (0, 0, qblk(b, j))),
                  _resident((n, A_QK), lambda b, j: (b, 0)),
                  pl.BlockSpec((m, A_QK), lambda b, j: (t_lat // m + b, 0)),
                  _resident((n // VB, DV, VB), lambda b, j: (b, 0, 0)),
                  pl.BlockSpec((m // VB, DV, VB), lambda b, j: (t_lat // m + b, 0, 0)),
                  pl.BlockSpec((A_HEADS, A_V, A_KV_RANK), lambda b, j: (0, 0, 0))],
        out_specs=pl.BlockSpec((A_HEADS * A_V, TQ), lambda b, j: (0, qblk(b, j))),
        out_shape=jax.ShapeDtypeStruct((A_HEADS * A_V, t_all), BF16),
        scratch_shapes=_flash_scratch(),
        compiler_params=_cparams(2),
        name="mla_attention",
    )(qa, ka, ka, ct, ct, wuvt)


def _diff_kernel(qb_ref, kl_ref, kc_ref, vl_ref, vc_ref, lam_ref, gsub_ref, o_ref, *scratch, n_lat_steps, n_qb_lat,
                 lam_init):
    fls = _flash_from_scratch(scratch)
    is_latent = pl.program_id(1) < n_qb_lat
    lv = lam_ref[...]
    lam = (jnp.exp(jnp.sum(lv[0:1] * lv[1:2], axis=1, keepdims=True))
           - jnp.exp(jnp.sum(lv[2:3] * lv[3:4], axis=1, keepdims=True)) + lam_init)

    def group(gg, carry):
        hds = [ITEMS * gg + u for u in range(ITEMS)]
        q_ts = [_block_diag_queries_t(qb_ref[hd]) for hd in hds]
        ctx_block = lambda hd: (kc_ref[hd], [vc_ref[hd, 0]])
        lat_block = lambda hd, t: (kl_ref[hd, TK * t:TK * (t + 1), :],
                                   [vl_ref[hd, (TK // VB) * t + i] for i in range(TK // VB)])

        @pl.when(is_latent)
        def _():
            for fl, q_t, hd in zip(fls, q_ts, hds):
                fl.run(q_t, [lat_block(hd, t) for t in range(n_lat_steps)] + [ctx_block(hd)], KEY_GROUP)

        @pl.when(jnp.logical_not(is_latent))
        def _():
            for fl, q_t, hd in zip(fls, q_ts, hds):
                fl.run(q_t, [ctx_block(hd)], KEY_GROUP)

        for fl, hd in zip(fls, hds):
            on = fl.normalised()
            o = on[:, 0:TQ] - lam * on[:, TQ:2 * TQ]
            o = o * lax.rsqrt(jnp.mean(o * o, axis=0, keepdims=True) + SUBLN_EPS) * gsub_ref[...]
            o_ref[pl.ds(pl.multiple_of(hd * LANES, LANES), LANES), :] = (o * (1.0 - lam_init)).astype(BF16)
        return carry

    lax.fori_loop(0, B_HEADS // ITEMS, group, 0)


def _diff(qb, kb, vbt, lam_vec, gsub_col, lam_init, bsz, n, m):
    t_lat = bsz * n
    t_all = t_lat + bsz * m
    nqb = n // TQ
    qblk = lambda b, j: jnp.where(j < nqb, b * nqb + j, t_lat // TQ + b)
    kern = functools.partial(_diff_kernel, n_lat_steps=n // TK, n_qb_lat=nqb, lam_init=lam_init)
    return pl.pallas_call(
        kern,
        grid=(bsz, nqb + 1),
        in_specs=[pl.BlockSpec((B_HEADS, LANES, TQ), lambda b, j: (0, 0, qblk(b, j))),
                  _resident((B_HEADS, n, LANES), lambda b, j: (0, b, 0)),
                  pl.BlockSpec((B_HEADS, m, LANES), lambda b, j: (0, t_lat // m + b, 0)),
                  _resident((B_HEADS, n // VB, DV, VB), lambda b, j: (0, b, 0, 0)),
                  pl.BlockSpec((B_HEADS, m // VB, DV, VB), lambda b, j: (0, t_lat // m + b, 0, 0)),
                  pl.BlockSpec((4, B_HD), lambda b, j: (0, 0)),
                  pl.BlockSpec((2 * B_HD, 1), lambda b, j: (0, 0))],
        out_specs=pl.BlockSpec((B_HEADS * LANES, TQ), lambda b, j: (0, qblk(b, j))),
        out_shape=jax.ShapeDtypeStruct((B_HEADS * LANES, t_all), BF16),
        scratch_shapes=_flash_scratch(),
        compiler_params=_cparams(2),
        name="diff_attention",
    )(qb, kb, kb, vbt, vbt, lam_vec, gsub_col)


def _odd_proj_kernel(h_ref, mod_ref, win_ref, u_ref, q_ref, k_ref, vt_ref):
    h = h_ref[...]
    hm = (h * (1.0 + mod_ref[0, 1:2, :]) + mod_ref[0, 0:1, :]).astype(BF16)
    p = _dot(hm, win_ref[...])
    u_ref[...] = p[:, 0:C_WIDTH]
    ones_rows = _ones_rows(TM)
    for s in range(D_WIDTH // LANES):
        sl = slice(LANES * s, LANES * (s + 1))
        q_ref[s] = (p[:, C_WIDTH:C_WIDTH + D_WIDTH][:, sl].T * D_QSCALE).astype(BF16)
        k_ref[s] = p[:, C_WIDTH + D_WIDTH:C_WIDTH + 2 * D_WIDTH][:, sl].astype(BF16)
        vt = p[:, C_WIDTH + 2 * D_WIDTH:C_WIDTH + 3 * D_WIDTH][:, sl].T.astype(BF16)
        vt = jnp.concatenate([vt, ones_rows], axis=0)
        for i in range(TM // VB):
            vt_ref[s, i] = vt[:, VB * i:VB * (i + 1)]


def _odd_proj(h, mod3, win, n_tiles, mod_index):
    t_all = n_tiles * TM
    slabs = D_WIDTH // LANES
    return pl.pallas_call(
        _odd_proj_kernel,
        grid=(n_tiles,),
        in_specs=[pl.BlockSpec((TM, D_MODEL), lambda i: (i, 0)),
                  pl.BlockSpec((1, 3, D_MODEL), lambda i: (mod_index(i), 0, 0)),
                  _resident((D_MODEL, C_WIDTH + 3 * D_WIDTH), lambda i: (0, 0))],
        out_specs=[pl.BlockSpec((TM, C_WIDTH), lambda i: (i, 0)),
                   pl.BlockSpec((slabs, LANES, TM), lambda i: (0, 0, i)),
                   pl.BlockSpec((slabs, TM, LANES), lambda i: (0, i, 0)),
                   pl.BlockSpec((slabs, TM // VB, DV, VB), lambda i: (0, i, 0, 0))],
        out_shape=[jax.ShapeDtypeStruct((t_all, C_WIDTH), F32),
                   jax.ShapeDtypeStruct((slabs, LANES, t_all), BF16),
                   jax.ShapeDtypeStruct((slabs, t_all, LANES), BF16),
                   jax.ShapeDtypeStruct((slabs, t_all // VB, DV, VB), BF16)],
        compiler_params=_cparams(1),
        name="odd_proj",
    )(h, mod3, win)


def _pool_kernel(prev_ref, u_ref, next_ref, w_ref, ps_ref, o_ref, ext_ref, *, tiles_per_seq):
    t = pl.program_id(0) % tiles_per_seq
    n = tiles_per_seq * TM
    ext_ref[0:POOL_HALO, :] = jnp.where(t > 0, prev_ref[...], 0.0)
    ext_ref[POOL_HALO:POOL_HALO + TM, :] = u_ref[...]
    ext_ref[POOL_HALO + TM:2 * POOL_HALO + TM, :] = jnp.where(t < tiles_per_seq - 1, next_ref[...], 0.0)
    pos = t * TM + lax.broadcasted_iota(jnp.int32, (TM, 1), 0)
    for g, w in enumerate(C_WINDOWS):
        left = w // 2
        right = w - 1 - left
        lanes = slice(C_GW * g, C_GW * (g + 1))
        tot = ext_ref[POOL_HALO - left:POOL_HALO - left + TM, lanes]
        for k in range(1 - left, right + 1):
            tot = tot + ext_ref[POOL_HALO + k:POOL_HALO + k + TM, lanes]
        cnt = jnp.minimum(pos + right + 1, n) - jnp.maximum(pos - left, 0)
        pooled = tot / cnt.astype(F32) - u_ref[:, lanes]
        y = _dot(pooled.astype(BF16), w_ref[g])
        o_ref[:, lanes] = (y * ps_ref[:, lanes]).astype(BF16)


def _pool(u, w_pool, pool_scale, n_tiles, tiles_per_seq):
    per = TM // POOL_HALO
    last = u.shape[0] // POOL_HALO - 1
    return pl.pallas_call(
        functools.partial(_pool_kernel, tiles_per_seq=tiles_per_seq),
        grid=(n_tiles,),
        in_specs=[pl.BlockSpec((POOL_HALO, C_WIDTH), lambda i: (jnp.maximum(i * per - 1, 0), 0)),
                  pl.BlockSpec((TM, C_WIDTH), lambda i: (i, 0)),
                  pl.BlockSpec((POOL_HALO, C_WIDTH), lambda i: (jnp.minimum((i + 1) * per, last), 0)),
                  pl.BlockSpec((C_GROUPS, C_GW, C_GW), lambda i: (0, 0, 0)),
                  pl.BlockSpec((1, C_WIDTH), lambda i: (0, 0))],
        out_specs=pl.BlockSpec((TM, C_WIDTH), lambda i: (i, 0)),
        out_shape=jax.ShapeDtypeStruct((n_tiles * TM, C_WIDTH), BF16),
        scratch_shapes=[pltpu.VMEM((TM + 2 * POOL_HALO, C_WIDTH), F32)],
        compiler_params=_cparams(1),
        name="multiscale_pool",
    )(u, u, u, w_pool, pool_scale)


def _na_row_table(rows):
    qr_per = TQ // GRID_W
    nblk = rows // qr_per
    g_of = (0, 1, nblk - 1)
    gb_of = tuple(min(max(g - 1, 0), nblk - 3) for g in g_of)
    table = {}
    for p in range(NA_PATTERNS):
        for kr in range(NA_KEY_ROWS):
            for qr in range(qr_per):
                r = qr_per * g_of[p] + qr
                ka = qr_per * gb_of[p] + kr
                r0 = min(max(r - NA_ROWS // 2, 0), rows - NA_ROWS)
                table[p, kr, qr] = (r0 <= ka < r0 + NA_ROWS, ka - r + NA_ROWS - 1)
    return table


def _na_bias_kernel(rpb_ref, o_ref, *, rows):
    hd = pl.program_id(0)
    n_r, n_c = 2 * NA_ROWS - 1, 2 * NA_COLS - 1
    qr_per = TQ // GRID_W
    kc = lax.broadcasted_iota(jnp.int32, (GRID_W, TQ), 0)
    q = lax.broadcasted_iota(jnp.int32, (GRID_W, TQ), 1)
    qc, qrow = q % GRID_W, q // GRID_W
    idx = kc - qc + NA_COLS - 1
    cs = jnp.clip(qc - NA_COLS // 2, 0, GRID_W - NA_COLS)
    col_ok = (kc >= cs) & (kc < cs + NA_COLS)
    neg = jnp.full((GRID_W, TQ), NEG_INF, F32)
    toeplitz = []
    for dr in range(n_r):
        acc = neg
        for j in range(n_c):
            acc = jnp.where(idx == j, rpb_ref[hd * n_r * n_c + dr * n_c + j], acc)
        toeplitz.append(jnp.where(col_ok, acc * LOG2E, neg))
    table = _na_row_table(rows)
    for p in range(NA_PATTERNS):
        for kr in range(NA_KEY_ROWS):
            blk = neg
            for qr in range(qr_per):
                ok, ridx = table[p, kr, qr]
                if ok:
                    blk = jnp.where(qrow == qr, toeplitz[ridx], blk)
            o_ref[p, 0, GRID_W * kr:GRID_W * (kr + 1), :] = blk


def _na_bias(rpb, rows):
    n_r, n_c = 2 * NA_ROWS - 1, 2 * NA_COLS - 1
    n_keys = NA_KEY_ROWS * GRID_W
    return pl.pallas_call(
        functools.partial(_na_bias_kernel, rows=rows),
        grid=(D_HEADS,),
        in_specs=[pl.BlockSpec(memory_space=pltpu.SMEM)],
        out_specs=pl.BlockSpec((NA_PATTERNS, 1, n_keys, TQ), lambda i: (0, i // 2, 0, i % 2)),
        out_shape=jax.ShapeDtypeStruct((NA_PATTERNS, D_HEADS // 2, n_keys, 2 * TQ), F32),
        compiler_params=_cparams(1),
        name="na_bias",
    )(rpb.reshape(D_HEADS * n_r * n_c))


def _na_kernel(q_ref, kl_ref, kc_ref, vl_ref, vc_ref, bias_ref, o_ref, *scratch, nqb):
    s_refs, p_refs, accs = scratch[0:N_SLOTS], scratch[N_SLOTS:2 * N_SLOTS], scratch[2 * N_SLOTS:]
    n_blk = NA_KEY_ROWS * GRID_W // VB
    half = LANES // 2
    for i in range(NA_GS):
        fl = _Flash(s_refs, p_refs, accs[i % len(accs)], runtime_offsets=True)
        g = pl.program_id(2) * NA_GS + i
        gb = jnp.clip(g - 1, 0, nqb - n_blk)
        pat = jnp.where(g == 0, 0, jnp.where(g == nqb - 1, 2, 1))
        cols = slice(TQ * i, TQ * (i + 1))
        q_t = _block_diag_queries_t(q_ref[0, :, cols])
        blocks = [(kl_ref[0, pl.ds(pl.multiple_of((gb + j) * VB, VB), VB), :], [vl_ref[0, gb + j]],
                   bias_ref[pat, 0, VB * j:VB * (j + 1), :]) for j in range(n_blk)]
        blocks.append((kc_ref[0], [vc_ref[0, 0]]))
        fl.run(q_t, blocks, len(blocks), first=len(blocks) * i)
        on = fl.normalised()
        o_ref[0:half, cols] = on[0:half, 0:TQ].astype(BF16)
        o_ref[half:LANES, cols] = on[half:LANES, TQ:2 * TQ].astype(BF16)


def _natten(qn, kn, vnt, bias, bsz, n, m):
    t_lat = bsz * n
    nqb = n // TQ
    steps = nqb // NA_GS
    slabs = D_WIDTH // LANES
    return pl.pallas_call(
        functools.partial(_na_kernel, nqb=nqb),
        grid=(bsz, slabs, steps),
        in_specs=[pl.BlockSpec((1, LANES, NA_GS * TQ), lambda b, s, g: (s, 0, b * steps + g)),
                  pl.BlockSpec((1, n, LANES), lambda b, s, g: (s, b, 0)),
                  pl.BlockSpec((1, m, LANES), lambda b, s, g: (s, t_lat // m + b, 0)),
                  pl.BlockSpec((1, n // VB, DV, VB), lambda b, s, g: (s, b, 0, 0)),
                  pl.BlockSpec((1, m // VB, DV, VB), lambda b, s, g: (s, t_lat // m + b, 0, 0)),
                  pl.BlockSpec((NA_PATTERNS, 1, NA_KEY_ROWS * GRID_W, 2 * TQ), lambda b, s, g: (0, s, 0, 0))],
        out_specs=pl.BlockSpec((LANES, NA_GS * TQ), lambda b, s, g: (s, b * steps + g)),
        out_shape=jax.ShapeDtypeStruct((D_WIDTH, t_lat), BF16),
        scratch_shapes=([pltpu.VMEM((VB, NQ), F32)] * N_SLOTS + [pltpu.VMEM((VB, NQ), BF16)] * N_SLOTS
                        + [pltpu.VMEM((DV, NQ), F32)] * 2),
        compiler_params=_cparams(3),
        name="neighbourhood_attention",
    )(qn, kn, kn, vnt, vnt, bias)


def _rope_tables(n, rot_dim):
    t = np.arange(n)
    row = (t // GRID_W).astype(np.float32)
    col = (t % GRID_W).astype(np.float32)
    quarter = rot_dim // 4
    inv = jnp.asarray(ROPE_BASE, F32) ** (-jnp.arange(quarter, dtype=F32) / quarter)
    ang_r = jnp.asarray(row)[:, None] * inv[None, :]
    ang_c = jnp.asarray(col)[:, None] * inv[None, :]
    ang = jnp.concatenate([ang_r, ang_r, ang_c, ang_c], -1)
    sign = np.where((np.arange(rot_dim) % (2 * quarter)) < quarter, -1.0, 1.0).astype(np.float32)
    cos = jnp.tile(jnp.cos(ang), (1, LANES // rot_dim))
    sin = jnp.tile(jnp.sin(ang) * sign[None, :], (1, LANES // rot_dim))
    cos = jnp.concatenate([cos, jnp.ones((TM, LANES), F32)], 0)
    sin = jnp.concatenate([sin, jnp.zeros((TM, LANES), F32)], 0)
    return cos, sin


def _even_weights(w_in, w_uq, w_ukv):
    o = A_Q_RANK + B_QK
    kpe = jnp.pad(w_in[:, o + A_KV_RANK:o + A_KV_RANK + A_ROPE], ((0, 0), (0, LANES - A_ROPE)))
    win = jnp.concatenate([w_in[:, 0:A_Q_RANK], w_in[:, A_Q_RANK:o], w_in[:, o:o + A_KV_RANK], kpe,
                           w_in[:, o + A_KV_RANK + A_ROPE:o + A_KV_RANK + A_ROPE + B_QK],
                           w_in[:, o + A_KV_RANK + A_ROPE + B_QK:]], axis=1).astype(BF16)
    uq = w_uq.reshape(A_Q_RANK, A_HEADS, A_NOPE + A_ROPE)
    wuq = jnp.concatenate([uq[:, :, :A_NOPE].reshape(A_Q_RANK, A_HEADS * A_NOPE),
                           uq[:, :, A_NOPE:].reshape(A_Q_RANK, A_HEADS * A_ROPE)], axis=1).astype(BF16)
    ukv = w_ukv.reshape(A_KV_RANK, A_HEADS, A_NOPE + A_V)
    uk_t = jnp.transpose(ukv[:, :, :A_NOPE], (1, 2, 0))
    wabs = jnp.zeros((A_HEADS, A_NOPE, A_HEADS, A_KV_RANK), F32)
    wabs = wabs.at[np.arange(A_HEADS), :, np.arange(A_HEADS), :].set(uk_t)
    wabs = wabs.reshape(A_HEADS * A_NOPE, A_HEADS * A_KV_RANK).astype(BF16)
    wuvt = jnp.transpose(ukv[:, :, A_NOPE:], (1, 2, 0)).astype(BF16)
    return win, wuq, wabs, wuvt


def kernel(x, c, ctx, c_ctx, ada_w, ada_b, ln_g, ln_b, ffn_w_gate, ffn_w_up, ffn_w_down, ev_w_in, ev_w_out,
           ev_g_qlat, ev_g_kvlat, ev_w_uq, ev_w_ukv, ev_lam, ev_g_sub, od_w_in, od_w_out, od_w_pool,
           od_pool_scale, od_rpb):
    bsz, n, d = x.shape
    m = ctx.shape[1]
    assert d == D_MODEL and ada_w.shape[0] == DEPTH == 2
    assert m == TQ and n % FFN_ROWS == 0 and n % GRID_W == 0 and (bsz * m) % FFN_ROWS == 0 and bsz + 1 <= 8
    assert n // TQ >= 3 and (n // TQ) % NA_GS == 0
    t_lat = bsz * n
    t_all = t_lat + bsz * m
    tiles_lat, tiles_all, tiles_seq = t_lat // TM, t_all // TM, n // TM
    mod_index = lambda i: jnp.minimum(i // tiles_seq, bsz)
    tab_index = lambda i: jnp.where(i < tiles_lat, i % tiles_seq, tiles_seq)

    c_all = jnp.zeros((8, D_MODEL), F32).at[:bsz].set(c).at[bsz].set(c_ctx)
    mod = _modulation(c_all, ada_w, ada_b).reshape(DEPTH, 8, N_MOD, D_MODEL)
    row = lambda v: v.reshape(1, -1)
    ffn_w = lambda l, k: (ffn_w_gate[l, k].astype(BF16), ffn_w_up[l, k].astype(BF16), ffn_w_down[l, k].astype(BF16))

    h = _ffn(x.reshape(t_lat, D_MODEL), mod[0, :, 0:3], *ffn_w(0, 0), row(ln_g[0, 0]), row(ln_b[0, 0]),
             t_all, n, bsz, h_tail=ctx.reshape(bsz * m, D_MODEL))
    win, wuq, wabs, wuvt = _even_weights(ev_w_in[0], ev_w_uq[0], ev_w_ukv[0])
    tabs = _rope_tables(n, A_ROPE) + _rope_tables(n, B_HD)
    qa, ka, ct, qb, kb, vbt = _even_proj(h, mod[0, :, 3:6], win, row(ev_g_qlat[0]), row(ev_g_kvlat[0]), wuq, wabs,
                                         tabs, tiles_all, mod_index, tab_index)
    lam_init = 0.8 - 0.6 * math.exp(-0.3 * 0)
    mix_a = _mla(qa, ka, ct, wuvt, bsz, n, m)
    mix_b = _diff(qb, kb, vbt, ev_lam[0], ev_g_sub[0].reshape(2 * B_HD, 1), lam_init, bsz, n, m)
    h = _ffn(h, mod[0, :, 3:9], *ffn_w(0, 1), row(ln_g[0, 2]), row(ln_b[0, 2]), t_all, n, bsz,
             mixer=("tt", mix_a, mix_b, ev_w_out[0].astype(BF16), row(ln_g[0, 1]), row(ln_b[0, 1])))

    h = _ffn(h, mod[1, :, 0:3], *ffn_w(1, 0), row(ln_g[1, 0]), row(ln_b[1, 0]), t_all, n, bsz)
    u, qn, kn, vnt = _odd_proj(h, mod[1, :, 3:6], od_w_in[0].astype(BF16), tiles_all, mod_index)
    y_pool = _pool(u, od_w_pool[0].astype(BF16), row(od_pool_scale[0]), tiles_lat, tiles_seq)
    bias = _na_bias(od_rpb[0], n // GRID_W)
    y_na = _natten(qn, kn, vnt, bias, bsz, n, m)
    h = _ffn(h, mod[1, :, 3:9], *ffn_w(1, 1), row(ln_g[1, 2]), row(ln_b[1, 2]), t_lat, n, bsz,
             mixer=("nt", y_pool, y_na, od_w_out[0].astype(BF16), row(ln_g[1, 1]), row(ln_b[1, 1])))
    return h.reshape(bsz, n, D_MODEL)
```

```python
import functools
import math

import numpy as np
import jax
import jax.numpy as jnp
from jax import lax
from jax.experimental import pallas as pl
from jax.experimental.pallas import tpu as pltpu

F32 = jnp.float32
BF16 = jnp.bfloat16

D_MODEL = 1024
DEPTH = 2
GRID_W = 64
N_MOD = 9
DN_ALPHA = float((2 * DEPTH) ** 0.25)
LN_EPS = 1e-6
RMS_EPS = 1e-6
SUBLN_EPS = 1e-5
ROPE_BASE = 10000.0
NEG_INF = -1e30
D_FF = 2816

A_HEADS = 8
A_NOPE = 64
A_ROPE = 32
A_V = 64
A_Q_RANK = 256
A_KV_RANK = 128
A_SCALE = float((A_NOPE + A_ROPE) ** -0.5)
A_QK = A_KV_RANK + A_ROPE

B_HEADS = 4
B_HD = 64
B_SCALE = float(B_HD ** -0.5)
B_QK = B_HEADS * 2 * B_HD

C_GROUPS = 4
C_WINDOWS = (2, 4, 8, 16)
C_GW = 128
C_WIDTH = C_GROUPS * C_GW

D_HEADS = 8
D_HD = 64
D_SCALE = float(D_HD ** -0.5)
D_WIDTH = D_HEADS * D_HD
NA_ROWS = 8
NA_COLS = 16

LOG2E = math.log2(math.e)
A_QSCALE = A_SCALE * LOG2E
B_QSCALE = B_SCALE * LOG2E
D_QSCALE = D_SCALE * LOG2E

LANES = 128
SUBLANES = 8
TM = 512
FFN_ROWS = 2 * TM
TQ = 256
NQ = 2 * TQ
TK = 512
VB = 256
STRIP = 64
BF16_ROWS = 16
DV = LANES + BF16_ROWS
MXU_DIM = 256
FF_SPLITS = (0, 6 * MXU_DIM, D_FF)
POOL_HALO = 8
NA_KEY_ROWS = 12
NA_PATTERNS = 3
NA_GS = 8
VMEM_LIMIT = 56 * 1024 * 1024


def _cparams(n_axes):
    return pltpu.CompilerParams(dimension_semantics=("arbitrary",) * n_axes,
                                vmem_limit_bytes=VMEM_LIMIT)


def _resident(shape, index_map):
    return pl.BlockSpec(shape, index_map, pipeline_mode=pl.Buffered(1))


def _layer_norm(z, g, b):
    mu = jnp.mean(z, axis=-1, keepdims=True)
    zc = z - mu
    var = jnp.mean(zc * zc, axis=-1, keepdims=True)
    return zc * lax.rsqrt(var + LN_EPS) * g + b


def _rms_norm(x, g, eps):
    return x * lax.rsqrt(jnp.mean(x * x, axis=-1, keepdims=True) + eps) * g


def _dot(a, b):
    return jnp.dot(a, b, preferred_element_type=F32)


def _dot_tn(a, b):
    return lax.dot_general(a, b, (((0,), (0,)), ((), ())), preferred_element_type=F32)


def _mod_kernel(c_ref, w_ref, b_ref, o_ref):
    x = c_ref[...]
    s = (x * jax.nn.sigmoid(x)).astype(BF16)
    o_ref[0] = _dot(s, w_ref[0].astype(BF16)) + b_ref[0]


def _modulation(c_all, ada_w, ada_b):
    depth = ada_w.shape[0]
    return pl.pallas_call(
        _mod_kernel,
        grid=(depth, N_MOD),
        in_specs=[pl.BlockSpec((8, D_MODEL), lambda l, j: (0, 0)),
                  pl.BlockSpec((1, D_MODEL, D_MODEL), lambda l, j: (l, 0, j)),
                  pl.BlockSpec((1, 1, D_MODEL), lambda l, j: (l, 0, j))],
        out_specs=pl.BlockSpec((1, 8, D_MODEL), lambda l, j: (l, 0, j)),
        out_shape=jax.ShapeDtypeStruct((depth, 8, N_MOD * D_MODEL), F32),
        compiler_params=_cparams(2),
        name="modulation",
    )(c_all, ada_w, ada_b.reshape(depth, 1, N_MOD * D_MODEL))


def _ffn_kernel(*refs, n_first, mix):
    n_src = len(refs) - 7 - (5 if mix else 0)
    h_refs, (mod_ref, wg_ref, wu_ref, wd_ref, g_ref, b_ref, o_ref) = refs[:n_src], refs[-7:]
    off = 3 if mix else 0
    shift, scale, gate = mod_ref[0, off:off + 1, :], mod_ref[0, off + 1:off + 2, :], mod_ref[0, off + 2:off + 3, :]
    for part in range(FFN_ROWS // TM):
        rows = slice(TM * part, TM * (part + 1))
        h = h_refs[0][rows, :]
        if n_src == 2:
            h = jnp.where(pl.program_id(0) < n_first, h, h_refs[1][rows, :])
        if mix:
            a_ref, bm_ref, wo_ref, g1_ref, b1_ref = refs[n_src:n_src + 5]
            half = wo_ref.shape[0] // 2
            ya = (_dot_tn(a_ref[:, rows], wo_ref[0:half, :]) if mix[0] == "t"
                  else _dot(a_ref[rows, :], wo_ref[0:half, :]))
            y_mix = ya + _dot_tn(bm_ref[:, rows], wo_ref[half:2 * half, :])
            h = _layer_norm(DN_ALPHA * h + mod_ref[0, 2:3, :] * y_mix, g1_ref[...], b1_ref[...])
        hm = (h * (1.0 + scale) + shift).astype(BF16)
        y = jnp.zeros((TM, D_MODEL), F32)
        for c in range(len(FF_SPLITS) - 1):
            cols = slice(FF_SPLITS[c], FF_SPLITS[c + 1])
            gt = _dot(hm, wg_ref[:, cols])
            up = _dot(hm, wu_ref[:, cols])
            act = (gt * jax.nn.sigmoid(gt) * up).astype(BF16)
            y = y + _dot(act, wd_ref[cols, :])
        o_ref[rows, :] = _layer_norm(DN_ALPHA * h + (0.5 * gate) * y, g_ref[...], b_ref[...])


def _ffn(h, mods, wg, wu, wd, g, b, n_rows, seq_len, bsz, h_tail=None, mixer=None):
    n_first = h.shape[0] // FFN_ROWS
    mod_index = lambda i: jnp.minimum(i // (seq_len // FFN_ROWS), bsz)
    if h_tail is None:
        sources, src_specs = [h], [pl.BlockSpec((FFN_ROWS, D_MODEL), lambda i: (i, 0))]
    else:
        sources = [h, h_tail]
        src_specs = [pl.BlockSpec((FFN_ROWS, D_MODEL), lambda i: (jnp.minimum(i, n_first - 1), 0)),
                     pl.BlockSpec((FFN_ROWS, D_MODEL), lambda i: (jnp.maximum(i - n_first, 0), 0))]
    layouts = None
    if mixer is not None:
        layouts, mix_a, mix_b, w_out, g1, b1 = mixer
        half = w_out.shape[0] // 2
        t_spec = pl.BlockSpec((half, FFN_ROWS), lambda i: (0, i))
        n_spec = pl.BlockSpec((FFN_ROWS, half), lambda i: (i, 0))
        sources += [mix_a, mix_b, w_out, g1, b1]
        src_specs += [t_spec if layouts[0] == "t" else n_spec, t_spec,
                      _resident((2 * half, D_MODEL), lambda i: (0, 0)),
                      pl.BlockSpec((1, D_MODEL), lambda i: (0, 0)), pl.BlockSpec((1, D_MODEL), lambda i: (0, 0))]
    return pl.pallas_call(
        functools.partial(_ffn_kernel, n_first=n_first, mix=layouts),
        grid=(n_rows // FFN_ROWS,),
        in_specs=src_specs + [
                  pl.BlockSpec((1, mods.shape[1], D_MODEL), lambda i: (mod_index(i), 0, 0)),
                  _resident((D_MODEL, D_FF), lambda i: (0, 0)),
                  _resident((D_MODEL, D_FF), lambda i: (0, 0)),
                  _resident((D_FF, D_MODEL), lambda i: (0, 0)),
                  pl.BlockSpec((1, D_MODEL), lambda i: (0, 0)),
                  pl.BlockSpec((1, D_MODEL), lambda i: (0, 0))],
        out_specs=pl.BlockSpec((FFN_ROWS, D_MODEL), lambda i: (i, 0)),
        out_shape=jax.ShapeDtypeStruct((n_rows, D_MODEL), F32),
        compiler_params=_cparams(1),
        name="ffn_postnorm",
    )(*sources, mods, wg, wu, wd, g, b)


def _ones_rows(n):
    row = lax.broadcasted_iota(jnp.int32, (BF16_ROWS, n), 0)
    return jnp.where(row == 0, 1.0, 0.0).astype(BF16)


def _rope(x, cos, sin_signed, quarter):
    lane = lax.broadcasted_iota(jnp.int32, x.shape, 1)
    first = (lane % (2 * quarter)) < quarter
    rot = jnp.where(first, pltpu.roll(x, LANES - quarter, 1), pltpu.roll(x, quarter, 1))
    return x * cos + rot * sin_signed


EV_QLAT = 0
EV_BQ = 256
EV_KVLAT = 768
EV_KPE = 896
EV_BK = 1024
EV_BV = 1536
EV_PCOLS = 2048


def _even_proj_kernel(h_ref, mod_ref, win_ref, gq_ref, gkv_ref, wuq_ref, wabs_ref,
                      cosa_ref, sina_ref, cosb_ref, sinb_ref,
                      qa_ref, ka_ref, ct_ref, qb_ref, kb_ref, vbt_ref):
    h = h_ref[...]
    hm = (h * (1.0 + mod_ref[0, 1:2, :]) + mod_ref[0, 0:1, :]).astype(BF16)
    p = _dot(hm, win_ref[...])
    cosa, sina, cosb, sinb = cosa_ref[...], sina_ref[...], cosb_ref[...], sinb_ref[...]

    qn = _rms_norm(p[:, EV_QLAT:EV_QLAT + A_Q_RANK], gq_ref[...], RMS_EPS).astype(BF16)
    q = _dot(qn, wuq_ref[...])
    n_nope = A_HEADS * A_NOPE
    qabs = _dot(q[:, :n_nope].astype(BF16), wabs_ref[...])
    qpe = [_rope(q[:, n_nope + LANES * s:n_nope + LANES * (s + 1)], cosa, sina, A_ROPE // 4) for s in range(2)]
    qpe_t = jnp.concatenate(qpe, axis=1).T
    for hd in range(A_HEADS):
        qa_ref[hd, 0:A_KV_RANK, :] = (qabs[:, A_KV_RANK * hd:A_KV_RANK * (hd + 1)].T * A_QSCALE).astype(BF16)
        qa_ref[hd, A_KV_RANK:A_QK, :] = (qpe_t[A_ROPE * hd:A_ROPE * (hd + 1), :] * A_QSCALE).astype(BF16)

    cn = _rms_norm(p[:, EV_KVLAT:EV_KVLAT + A_KV_RANK], gkv_ref[...], RMS_EPS)
    kpe = _rope(p[:, EV_KPE:EV_KPE + LANES], cosa, sina, A_ROPE // 4)
    ka_ref[:, 0:A_KV_RANK] = cn.astype(BF16)
    ka_ref[:, A_KV_RANK:A_QK] = kpe[:, 0:A_ROPE].astype(BF16)
    ones_rows = _ones_rows(TM)
    cnt = jnp.concatenate([cn.T.astype(BF16), ones_rows], axis=0)
    for s in range(TM // VB):
        ct_ref[s] = cnt[:, VB * s:VB * (s + 1)]

    for hd in range(B_HEADS):
        sl = slice(LANES * hd, LANES * (hd + 1))
        qb_ref[hd] = (_rope(p[:, EV_BQ:EV_BQ + B_QK][:, sl], cosb, sinb, B_HD // 4).T * B_QSCALE).astype(BF16)
        kb_ref[hd] = _rope(p[:, EV_BK:EV_BK + B_QK][:, sl], cosb, sinb, B_HD // 4).astype(BF16)
        vt = jnp.concatenate([p[:, EV_BV:EV_BV + B_QK][:, sl].T.astype(BF16), ones_rows], axis=0)
        for s in range(TM // VB):
            vbt_ref[hd, s] = vt[:, VB * s:VB * (s + 1)]


def _even_proj(h, mod3, win, gq, gkv, wuq, wabs, tabs, n_tiles, mod_index, tab_index):
    t_all = n_tiles * TM
    row = lambda i: (i, 0)
    tab = lambda i: (tab_index(i), 0)
    const = lambda i: (0, 0)
    return pl.pallas_call(
        _even_proj_kernel,
        grid=(n_tiles,),
        in_specs=[pl.BlockSpec((TM, D_MODEL), row),
                  pl.BlockSpec((1, 3, D_MODEL), lambda i: (mod_index(i), 0, 0)),
                  _resident((D_MODEL, EV_PCOLS), const),
                  pl.BlockSpec((1, A_Q_RANK), const),
                  pl.BlockSpec((1, A_KV_RANK), const),
                  _resident((A_Q_RANK, A_HEADS * (A_NOPE + A_ROPE)), const),
                  _resident((A_HEADS * A_NOPE, A_HEADS * A_KV_RANK), const),
                  pl.BlockSpec((TM, LANES), tab), pl.BlockSpec((TM, LANES), tab),
                  pl.BlockSpec((TM, LANES), tab), pl.BlockSpec((TM, LANES), tab)],
        out_specs=[pl.BlockSpec((A_HEADS, A_QK, TM), lambda i: (0, 0, i)),
                   pl.BlockSpec((TM, A_QK), row),
                   pl.BlockSpec((TM // VB, DV, VB), lambda i: (i, 0, 0)),
                   pl.BlockSpec((B_HEADS, LANES, TM), lambda i: (0, 0, i)),
                   pl.BlockSpec((B_HEADS, TM, LANES), lambda i: (0, i, 0)),
                   pl.BlockSpec((B_HEADS, TM // VB, DV, VB), lambda i: (0, i, 0, 0))],
        out_shape=[jax.ShapeDtypeStruct((A_HEADS, A_QK, t_all), BF16),
                   jax.ShapeDtypeStruct((t_all, A_QK), BF16),
                   jax.ShapeDtypeStruct((t_all // VB, DV, VB), BF16),
                   jax.ShapeDtypeStruct((B_HEADS, LANES, t_all), BF16),
                   jax.ShapeDtypeStruct((B_HEADS, t_all, LANES), BF16),
                   jax.ShapeDtypeStruct((B_HEADS, t_all // VB, DV, VB), BF16)],
        compiler_params=_cparams(1),
        name="even_proj",
    )(h, mod3, win, gq, gkv, wuq, wabs, *tabs)


ITEMS = 1
N_SLOTS = 8
KEY_GROUP = 2


class _Flash:
    def __init__(self, s_refs, p_refs, acc_ref, runtime_offsets=False, l_ref=None):
        self.s, self.p, self.acc, self.l = s_refs, p_refs, acc_ref, l_ref
        self.dz = jnp.minimum(pl.program_id(0), 0) if runtime_offsets else None

    def rows(self, start, size):
        if self.dz is None:
            return slice(start, start + size)
        return pl.ds(pl.multiple_of(self.dz + start, BF16_ROWS), size)

    def qk(self, b, k, q_t, bias=None):
        rows = k.shape[0]
        s = _dot(k, q_t)
        if bias is not None:
            s = s + bias
        self.s[b % N_SLOTS][self.rows(0, rows), :] = s
        return jnp.max(s.reshape(rows // SUBLANES, SUBLANES, NQ), axis=0)

    def sm(self, blocks, rows, maxima, m_old):
        m8 = maxima[0]
        for x in maxima[1:]:
            m8 = jnp.maximum(m8, x)
        m_new = jnp.maximum(m_old, jnp.max(m8, axis=0, keepdims=True))
        alpha = jnp.exp2(m_old - m_new)
        l8 = None
        for b in blocks:
            s_ref, p_ref = self.s[b % N_SLOTS], self.p[b % N_SLOTS]
            for r in range(rows // STRIP):
                rr = self.rows(STRIP * r, STRIP)
                p = jnp.exp2(s_ref[rr, :] - m_new)
                p_ref[rr, :] = p.astype(BF16)
                if self.l is not None:
                    part = jnp.sum(p.reshape(STRIP // SUBLANES, SUBLANES, NQ), axis=0)
                    l8 = part if l8 is None else l8 + part
        if self.l is not None:
            self.l[...] = alpha * self.l[...] + l8
        return m_new, alpha

    def pv(self, blocks, vt_lists, alpha):
        pv = None
        n_out = DV if self.l is None else LANES
        for b, vts in zip(blocks, vt_lists):
            for i, vt in enumerate(vts):
                d = _dot(vt[0:n_out], self.p[b % N_SLOTS][self.rows(VB * i, VB), :])
                pv = d if pv is None else pv + d
        self.acc[0:n_out, :] = pv if alpha is None else alpha * self.acc[0:n_out, :] + pv

    def run(self, q_t, key_blocks, group, first=0):
        groups, i = [], 0
        while i < len(key_blocks):
            j = i + 1
            while j < len(key_blocks) and j - i < group and key_blocks[j][0].shape == key_blocks[i][0].shape:
                j += 1
            groups.append(list(range(i, j)))
            i = j
        m = jnp.full((1, NQ), NEG_INF, F32)
        if self.l is not None:
            self.l[...] = jnp.zeros(self.l.shape, F32)
        maxima, alphas = {}, {}
        for h in range(len(groups) + 2):
            if h < len(groups):
                for b in groups[h]:
                    blk = key_blocks[b]
                    maxima[b] = self.qk(first + b, blk[0], q_t, blk[2] if len(blk) > 2 else None)
            if 0 <= h - 1 < len(groups):
                g = groups[h - 1]
                m, alphas[h - 1] = self.sm([first + b for b in g], key_blocks[g[0]][0].shape[0],
                                           [maxima[b] for b in g], m)
            if 0 <= h - 2 < len(groups):
                g = groups[h - 2]
                self.pv([first + b for b in g], [key_blocks[b][1] for b in g], alphas[h - 2] if h > 2 else None)

    def normalised(self):
        if self.l is not None:
            return self.acc[0:LANES, :] * (1.0 / jnp.sum(self.l[...], axis=0, keepdims=True))
        return self.acc[0:LANES, :] * (1.0 / self.acc[LANES:LANES + 1, :])


def _flash_scratch():
    per_item = ([pltpu.VMEM((TK, NQ), F32)] * N_SLOTS + [pltpu.VMEM((TK, NQ), BF16)] * N_SLOTS
                + [pltpu.VMEM((DV, NQ), F32), pltpu.VMEM((SUBLANES, NQ), F32)])
    return per_item * ITEMS


def _flash_from_scratch(refs):
    per = 2 * N_SLOTS + 2
    return [_Flash(r[0:N_SLOTS], r[N_SLOTS:2 * N_SLOTS], r[2 * N_SLOTS], l_ref=r[2 * N_SLOTS + 1])
            for r in (refs[per * u:per * (u + 1)] for u in range(ITEMS))]


def _block_diag_queries_t(q_t):
    row = lax.broadcasted_iota(jnp.int32, q_t.shape, 0)
    zero = jnp.zeros_like(q_t)
    half = LANES // 2
    return jnp.concatenate([jnp.where(row < half, q_t, zero), jnp.where(row >= half, q_t, zero)], axis=1)


def _mla_kernel(qa_ref, kl_ref, kc_ref, ctl_ref, ctc_ref, wuvt_ref, o_ref, *scratch, n_lat_steps, n_qb_lat):
    fls = _flash_from_scratch(scratch)
    is_latent = pl.program_id(1) < n_qb_lat
    ctx_block = lambda: (kc_ref[...], [ctc_ref[0]])
    lat_block = lambda t: (kl_ref[TK * t:TK * (t + 1), :], [ctl_ref[(TK // VB) * t + i] for i in range(TK // VB)])

    def group(gg, carry):
        gs = [ITEMS * gg + u for u in range(ITEMS)]
        q_ts = [jnp.concatenate([qa_ref[2 * g], qa_ref[2 * g + 1]], axis=1) for g in gs]

        @pl.when(is_latent)
        def _():
            for fl, q_t in zip(fls, q_ts):
                fl.run(q_t, [lat_block(t) for t in range(n_lat_steps)] + [ctx_block()], KEY_GROUP)

        @pl.when(jnp.logical_not(is_latent))
        def _():
            for fl, q_t in zip(fls, q_ts):
                fl.run(q_t, [ctx_block()], KEY_GROUP)

        for fl, g in zip(fls, gs):
            o_lat = fl.normalised().astype(BF16)
            for i in range(2):
                o = _dot(wuvt_ref[2 * g + i], o_lat[:, TQ * i:TQ * (i + 1)])
                o_ref[pl.ds(pl.multiple_of((2 * g + i) * A_V, A_V), A_V), :] = o.astype(BF16)
        return carry

    lax.fori_loop(0, A_HEADS // 2 // ITEMS, group, 0)


def _mla(qa, ka, ct, wuvt, bsz, n, m):
    t_lat = bsz * n
    t_all = t_lat + bsz * m
    nqb = n // TQ
    qblk = lambda b, j: jnp.where(j < nqb, b * nqb + j, t_lat // TQ + b)
    kern = functools.partial(_mla_kernel, n_lat_steps=n // TK, n_qb_lat=nqb)
    return pl.pallas_call(
        kern,
        grid=(bsz, nqb + 1),
        in_specs=[pl.BlockSpec((A_HEADS, A_QK, TQ), lambda b, j: (0, 0, qblk(b, j))),
                  _resident((n, A_QK), lambda b, j: (b, 0)),
                  pl.BlockSpec((m, A_QK), lambda b, j: (t_lat // m + b, 0)),
                  _resident((n // VB, DV, VB), lambda b, j: (b, 0, 0)),
                  pl.BlockSpec((m // VB, DV, VB), lambda b, j: (t_lat // m + b, 0, 0)),
                  pl.BlockSpec((A_HEADS, A_V, A_KV_RANK), lambda b, j: (0, 0, 0))],
        out_specs=pl.BlockSpec((A_HEADS * A_V, TQ), lambda b, j: (0, qblk(b, j))),
        out_shape=jax.ShapeDtypeStruct((A_HEADS * A_V, t_all), BF16),
        scratch_shapes=_flash_scratch(),
        compiler_params=_cparams(2),
        name="mla_attention",
    )(qa, ka, ka, ct, ct, wuvt)


def _diff_kernel(qb_ref, kl_ref, kc_ref, vl_ref, vc_ref, lam_ref, gsub_ref, o_ref, *scratch, n_lat_steps, n_qb_lat,
                 lam_init):
    fls = _flash_from_scratch(scratch)
    is_latent = pl.program_id(1) < n_qb_lat
    lv = lam_ref[...]
    lam = (jnp.exp(jnp.sum(lv[0:1] * lv[1:2], axis=1, keepdims=True))
           - jnp.exp(jnp.sum(lv[2:3] * lv[3:4], axis=1, keepdims=True)) + lam_init)

    def group(gg, carry):
        hds = [ITEMS * gg + u for u in range(ITEMS)]
        q_ts = [_block_diag_queries_t(qb_ref[hd]) for hd in hds]
        ctx_block = lambda hd: (kc_ref[hd], [vc_ref[hd, 0]])
        lat_block = lambda hd, t: (kl_ref[hd, TK * t:TK * (t + 1), :],
                                   [vl_ref[hd, (TK // VB) * t + i] for i in range(TK // VB)])

        @pl.when(is_latent)
        def _():
            for fl, q_t, hd in zip(fls, q_ts, hds):
                fl.run(q_t, [lat_block(hd, t) for t in range(n_lat_steps)] + [ctx_block(hd)], KEY_GROUP)

        @pl.when(jnp.logical_not(is_latent))
        def _():
            for fl, q_t, hd in zip(fls, q_ts, hds):
                fl.run(q_t, [ctx_block(hd)], KEY_GROUP)

        for fl, hd in zip(fls, hds):
            on = fl.normalised()
            o = on[:, 0:TQ] - lam * on[:, TQ:2 * TQ]
            o = o * lax.rsqrt(jnp.mean(o * o, axis=0, keepdims=True) + SUBLN_EPS) * gsub_ref[...]
            o_ref[pl.ds(pl.multiple_of(hd * LANES, LANES), LANES), :] = (o * (1.0 - lam_init)).astype(BF16)
        return carry

    lax.fori_loop(0, B_HEADS // ITEMS, group, 0)


def _diff(qb, kb, vbt, lam_vec, gsub_col, lam_init, bsz, n, m):
    t_lat = bsz * n
    t_all = t_lat + bsz * m
    nqb = n // TQ
    qblk = lambda b, j: jnp.where(j < nqb, b * nqb + j, t_lat // TQ + b)
    kern = functools.partial(_diff_kernel, n_lat_steps=n // TK, n_qb_lat=nqb, lam_init=lam_init)
    return pl.pallas_call(
        kern,
        grid=(bsz, nqb + 1),
        in_specs=[pl.BlockSpec((B_HEADS, LANES, TQ), lambda b, j: (0, 0, qblk(b, j))),
                  _resident((B_HEADS, n, LANES), lambda b, j: (0, b, 0)),
                  pl.BlockSpec((B_HEADS, m, LANES), lambda b, j: (0, t_lat // m + b, 0)),
                  _resident((B_HEADS, n // VB, DV, VB), lambda b, j: (0, b, 0, 0)),
                  pl.BlockSpec((B_HEADS, m // VB, DV, VB), lambda b, j: (0, t_lat // m + b, 0, 0)),
                  pl.BlockSpec((4, B_HD), lambda b, j: (0, 0)),
                  pl.BlockSpec((2 * B_HD, 1), lambda b, j: (0, 0))],
        out_specs=pl.BlockSpec((B_HEADS * LANES, TQ), lambda b, j: (0, qblk(b, j))),
        out_shape=jax.ShapeDtypeStruct((B_HEADS * LANES, t_all), BF16),
        scratch_shapes=_flash_scratch(),
        compiler_params=_cparams(2),
        name="diff_attention",
    )(qb, kb, kb, vbt, vbt, lam_vec, gsub_col)


def _odd_proj_kernel(h_ref, mod_ref, win_ref, u_ref, q_ref, k_ref, vt_ref):
    h = h_ref[...]
    hm = (h * (1.0 + mod_ref[0, 1:2, :]) + mod_ref[0, 0:1, :]).astype(BF16)
    p = _dot(hm, win_ref[...])
    u_ref[...] = p[:, 0:C_WIDTH]
    ones_rows = _ones_rows(TM)
    for s in range(D_WIDTH // LANES):
        sl = slice(LANES * s, LANES * (s + 1))
        q_ref[s] = (p[:, C_WIDTH:C_WIDTH + D_WIDTH][:, sl].T * D_QSCALE).astype(BF16)
        k_ref[s] = p[:, C_WIDTH + D_WIDTH:C_WIDTH + 2 * D_WIDTH][:, sl].astype(BF16)
        vt = p[:, C_WIDTH + 2 * D_WIDTH:C_WIDTH + 3 * D_WIDTH][:, sl].T.astype(BF16)
        vt = jnp.concatenate([vt, ones_rows], axis=0)
        for i in range(TM // VB):
            vt_ref[s, i] = vt[:, VB * i:VB * (i + 1)]


def _odd_proj(h, mod3, win, n_tiles, mod_index):
    t_all = n_tiles * TM
    slabs = D_WIDTH // LANES
    return pl.pallas_call(
        _odd_proj_kernel,
        grid=(n_tiles,),
        in_specs=[pl.BlockSpec((TM, D_MODEL), lambda i: (i, 0)),
                  pl.BlockSpec((1, 3, D_MODEL), lambda i: (mod_index(i), 0, 0)),
                  _resident((D_MODEL, C_WIDTH + 3 * D_WIDTH), lambda i: (0, 0))],
        out_specs=[pl.BlockSpec((TM, C_WIDTH), lambda i: (i, 0)),
                   pl.BlockSpec((slabs, LANES, TM), lambda i: (0, 0, i)),
                   pl.BlockSpec((slabs, TM, LANES), lambda i: (0, i, 0)),
                   pl.BlockSpec((slabs, TM // VB, DV, VB), lambda i: (0, i, 0, 0))],
        out_shape=[jax.ShapeDtypeStruct((t_all, C_WIDTH), F32),
                   jax.ShapeDtypeStruct((slabs, LANES, t_all), BF16),
                   jax.ShapeDtypeStruct((slabs, t_all, LANES), BF16),
                   jax.ShapeDtypeStruct((slabs, t_all // VB, DV, VB), BF16)],
        compiler_params=_cparams(1),
        name="odd_proj",
    )(h, mod3, win)


def _pool_kernel(prev_ref, u_ref, next_ref, w_ref, ps_ref, o_ref, ext_ref, *, tiles_per_seq):
    t = pl.program_id(0) % tiles_per_seq
    n = tiles_per_seq * TM
    ext_ref[0:POOL_HALO, :] = jnp.where(t > 0, prev_ref[...], 0.0)
    ext_ref[POOL_HALO:POOL_HALO + TM, :] = u_ref[...]
    ext_ref[POOL_HALO + TM:2 * POOL_HALO + TM, :] = jnp.where(t < tiles_per_seq - 1, next_ref[...], 0.0)
    pos = t * TM + lax.broadcasted_iota(jnp.int32, (TM, 1), 0)
    for g, w in enumerate(C_WINDOWS):
        left = w // 2
        right = w - 1 - left
        lanes = slice(C_GW * g, C_GW * (g + 1))
        tot = ext_ref[POOL_HALO - left:POOL_HALO - left + TM, lanes]
        for k in range(1 - left, right + 1):
            tot = tot + ext_ref[POOL_HALO + k:POOL_HALO + k + TM, lanes]
        cnt = jnp.minimum(pos + right + 1, n) - jnp.maximum(pos - left, 0)
        pooled = tot / cnt.astype(F32) - u_ref[:, lanes]
        y = _dot(pooled.astype(BF16), w_ref[g])
        o_ref[:, lanes] = (y * ps_ref[:, lanes]).astype(BF16)


def _pool(u, w_pool, pool_scale, n_tiles, tiles_per_seq):
    per = TM // POOL_HALO
    last = u.shape[0] // POOL_HALO - 1
    return pl.pallas_call(
        functools.partial(_pool_kernel, tiles_per_seq=tiles_per_seq),
        grid=(n_tiles,),
        in_specs=[pl.BlockSpec((POOL_HALO, C_WIDTH), lambda i: (jnp.maximum(i * per - 1, 0), 0)),
                  pl.BlockSpec((TM, C_WIDTH), lambda i: (i, 0)),
                  pl.BlockSpec((POOL_HALO, C_WIDTH), lambda i: (jnp.minimum((i + 1) * per, last), 0)),
                  pl.BlockSpec((C_GROUPS, C_GW, C_GW), lambda i: (0, 0, 0)),
                  pl.BlockSpec((1, C_WIDTH), lambda i: (0, 0))],
        out_specs=pl.BlockSpec((TM, C_WIDTH), lambda i: (i, 0)),
        out_shape=jax.ShapeDtypeStruct((n_tiles * TM, C_WIDTH), BF16),
        scratch_shapes=[pltpu.VMEM((TM + 2 * POOL_HALO, C_WIDTH), F32)],
        compiler_params=_cparams(1),
        name="multiscale_pool",
    )(u, u, u, w_pool, pool_scale)


def _na_row_table(rows):
    qr_per = TQ // GRID_W
    nblk = rows // qr_per
    g_of = (0, 1, nblk - 1)
    gb_of = tuple(min(max(g - 1, 0), nblk - 3) for g in g_of)
    table = {}
    for p in range(NA_PATTERNS):
        for kr in range(NA_KEY_ROWS):
            for qr in range(qr_per):
                r = qr_per * g_of[p] + qr
                ka = qr_per * gb_of[p] + kr
                r0 = min(max(r - NA_ROWS // 2, 0), rows - NA_ROWS)
                table[p, kr, qr] = (r0 <= ka < r0 + NA_ROWS, ka - r + NA_ROWS - 1)
    return table


def _na_bias_kernel(rpb_ref, o_ref, *, rows):
    hd = pl.program_id(0)
    n_r, n_c = 2 * NA_ROWS - 1, 2 * NA_COLS - 1
    qr_per = TQ // GRID_W
    kc = lax.broadcasted_iota(jnp.int32, (GRID_W, TQ), 0)
    q = lax.broadcasted_iota(jnp.int32, (GRID_W, TQ), 1)
    qc, qrow = q % GRID_W, q // GRID_W
    idx = kc - qc + NA_COLS - 1
    cs = jnp.clip(qc - NA_COLS // 2, 0, GRID_W - NA_COLS)
    col_ok = (kc >= cs) & (kc < cs + NA_COLS)
    neg = jnp.full((GRID_W, TQ), NEG_INF, F32)
    toeplitz = []
    for dr in range(n_r):
        acc = neg
        for j in range(n_c):
            acc = jnp.where(idx == j, rpb_ref[hd * n_r * n_c + dr * n_c + j], acc)
        toeplitz.append(jnp.where(col_ok, acc * LOG2E, neg))
    table = _na_row_table(rows)
    for p in range(NA_PATTERNS):
        for kr in range(NA_KEY_ROWS):
            blk = neg
            for qr in range(qr_per):
                ok, ridx = table[p, kr, qr]
                if ok:
                    blk = jnp.where(qrow == qr, toeplitz[ridx], blk)
            o_ref[p, 0, GRID_W * kr:GRID_W * (kr + 1), :] = blk


def _na_bias(rpb, rows):
    n_r, n_c = 2 * NA_ROWS - 1, 2 * NA_COLS - 1
    n_keys = NA_KEY_ROWS * GRID_W
    return pl.pallas_call(
        functools.partial(_na_bias_kernel, rows=rows),
        grid=(D_HEADS,),
        in_specs=[pl.BlockSpec(memory_space=pltpu.SMEM)],
        out_specs=pl.BlockSpec((NA_PATTERNS, 1, n_keys, TQ), lambda i: (0, i // 2, 0, i % 2)),
        out_shape=jax.ShapeDtypeStruct((NA_PATTERNS, D_HEADS // 2, n_keys, 2 * TQ), F32),
        compiler_params=_cparams(1),
        name="na_bias",
    )(rpb.reshape(D_HEADS * n_r * n_c))


def _na_kernel(q_ref, kl_ref, kc_ref, vl_ref, vc_ref, bias_ref, o_ref, *scratch, nqb):
    s_refs, p_refs, accs = scratch[0:N_SLOTS], scratch[N_SLOTS:2 * N_SLOTS], scratch[2 * N_SLOTS:]
    n_blk = NA_KEY_ROWS * GRID_W // VB
    half = LANES // 2
    for i in range(NA_GS):
        fl = _Flash(s_refs, p_refs, accs[i % len(accs)], runtime_offsets=True)
        g = pl.program_id(2) * NA_GS + i
        gb = jnp.clip(g - 1, 0, nqb - n_blk)
        pat = jnp.where(g == 0, 0, jnp.where(g == nqb - 1, 2, 1))
        cols = slice(TQ * i, TQ * (i + 1))
        q_t = _block_diag_queries_t(q_ref[0, :, cols])
        blocks = [(kl_ref[0, pl.ds(pl.multiple_of((gb + j) * VB, VB), VB), :], [vl_ref[0, gb + j]],
                   bias_ref[pat, 0, VB * j:VB * (j + 1), :]) for j in range(n_blk)]
        blocks.append((kc_ref[0], [vc_ref[0, 0]]))
        fl.run(q_t, blocks, len(blocks), first=len(blocks) * i)
        on = fl.normalised()
        o_ref[0:half, cols] = on[0:half, 0:TQ].astype(BF16)
        o_ref[half:LANES, cols] = on[half:LANES, TQ:2 * TQ].astype(BF16)


def _natten(qn, kn, vnt, bias, bsz, n, m):
    t_lat = bsz * n
    nqb = n // TQ
    steps = nqb // NA_GS
    slabs = D_WIDTH // LANES
    return pl.pallas_call(
        functools.partial(_na_kernel, nqb=nqb),
        grid=(bsz, slabs, steps),
        in_specs=[pl.BlockSpec((1, LANES, NA_GS * TQ), lambda b, s, g: (s, 0, b * steps + g)),
                  pl.BlockSpec((1, n, LANES), lambda b, s, g: (s, b, 0)),
                  pl.BlockSpec((1, m, LANES), lambda b, s, g: (s, t_lat // m + b, 0)),
                  pl.BlockSpec((1, n // VB, DV, VB), lambda b, s, g: (s, b, 0, 0)),
                  pl.BlockSpec((1, m // VB, DV, VB), lambda b, s, g: (s, t_lat // m + b, 0, 0)),
                  pl.BlockSpec((NA_PATTERNS, 1, NA_KEY_ROWS * GRID_W, 2 * TQ), lambda b, s, g: (0, s, 0, 0))],
        out_specs=pl.BlockSpec((LANES, NA_GS * TQ), lambda b, s, g: (s, b * steps + g)),
        out_shape=jax.ShapeDtypeStruct((D_WIDTH, t_lat), BF16),
        scratch_shapes=([pltpu.VMEM((VB, NQ), F32)] * N_SLOTS + [pltpu.VMEM((VB, NQ), BF16)] * N_SLOTS
                        + [pltpu.VMEM((DV, NQ), F32)] * 2),
        compiler_params=_cparams(3),
        name="neighbourhood_attention",
    )(qn, kn, kn, vnt, vnt, bias)


def _rope_tables(n, rot_dim):
    t = np.arange(n)
    row = (t // GRID_W).astype(np.float32)
    col = (t % GRID_W).astype(np.float32)
    quarter = rot_dim // 4
    inv = jnp.asarray(ROPE_BASE, F32) ** (-jnp.arange(quarter, dtype=F32) / quarter)
    ang_r = jnp.asarray(row)[:, None] * inv[None, :]
    ang_c = jnp.asarray(col)[:, None] * inv[None, :]
    ang = jnp.concatenate([ang_r, ang_r, ang_c, ang_c], -1)
    sign = np.where((np.arange(rot_dim) % (2 * quarter)) < quarter, -1.0, 1.0).astype(np.float32)
    cos = jnp.tile(jnp.cos(ang), (1, LANES // rot_dim))
    sin = jnp.tile(jnp.sin(ang) * sign[None, :], (1, LANES // rot_dim))
    cos = jnp.concatenate([cos, jnp.ones((TM, LANES), F32)], 0)
    sin = jnp.concatenate([sin, jnp.zeros((TM, LANES), F32)], 0)
    return cos, sin


def _even_weights(w_in, w_uq, w_ukv):
    o = A_Q_RANK + B_QK
    kpe = jnp.pad(w_in[:, o + A_KV_RANK:o + A_KV_RANK + A_ROPE], ((0, 0), (0, LANES - A_ROPE)))
    win = jnp.concatenate([w_in[:, 0:A_Q_RANK], w_in[:, A_Q_RANK:o], w_in[:, o:o + A_KV_RANK], kpe,
                           w_in[:, o + A_KV_RANK + A_ROPE:o + A_KV_RANK + A_ROPE + B_QK],
                           w_in[:, o + A_KV_RANK + A_ROPE + B_QK:]], axis=1).astype(BF16)
    uq = w_uq.reshape(A_Q_RANK, A_HEADS, A_NOPE + A_ROPE)
    wuq = jnp.concatenate([uq[:, :, :A_NOPE].reshape(A_Q_RANK, A_HEADS * A_NOPE),
                           uq[:, :, A_NOPE:].reshape(A_Q_RANK, A_HEADS * A_ROPE)], axis=1).astype(BF16)
    ukv = w_ukv.reshape(A_KV_RANK, A_HEADS, A_NOPE + A_V)
    uk_t = jnp.transpose(ukv[:, :, :A_NOPE], (1, 2, 0))
    wabs = jnp.zeros((A_HEADS, A_NOPE, A_HEADS, A_KV_RANK), F32)
    wabs = wabs.at[np.arange(A_HEADS), :, np.arange(A_HEADS), :].set(uk_t)
    wabs = wabs.reshape(A_HEADS * A_NOPE, A_HEADS * A_KV_RANK).astype(BF16)
    wuvt = jnp.transpose(ukv[:, :, A_NOPE:], (1, 2, 0)).astype(BF16)
    return win, wuq, wabs, wuvt


def kernel(x, c, ctx, c_ctx, ada_w, ada_b, ln_g, ln_b, ffn_w_gate, ffn_w_up, ffn_w_down, ev_w_in, ev_w_out,
           ev_g_qlat, ev_g_kvlat, ev_w_uq, ev_w_ukv, ev_lam, ev_g_sub, od_w_in, od_w_out, od_w_pool,
           od_pool_scale, od_rpb):
    bsz, n, d = x.shape
    m = ctx.shape[1]
    assert d == D_MODEL and ada_w.shape[0] == DEPTH == 2
    assert m == TQ and n % FFN_ROWS == 0 and n % GRID_W == 0 and (bsz * m) % FFN_ROWS == 0 and bsz + 1 <= 8
    assert n // TQ >= 3 and (n // TQ) % NA_GS == 0
    t_lat = bsz * n
    t_all = t_lat + bsz * m
    tiles_lat, tiles_all, tiles_seq = t_lat // TM, t_all // TM, n // TM
    mod_index = lambda i: jnp.minimum(i // tiles_seq, bsz)
    tab_index = lambda i: jnp.where(i < tiles_lat, i % tiles_seq, tiles_seq)

    c_all = jnp.zeros((8, D_MODEL), F32).at[:bsz].set(c).at[bsz].set(c_ctx)
    mod = _modulation(c_all, ada_w, ada_b).reshape(DEPTH, 8, N_MOD, D_MODEL)
    row = lambda v: v.reshape(1, -1)
    wg, wu, wd = ffn_w_gate.astype(BF16), ffn_w_up.astype(BF16), ffn_w_down.astype(BF16)

    h = _ffn(x.reshape(t_lat, D_MODEL), mod[0, :, 0:3], wg[0, 0], wu[0, 0], wd[0, 0], row(ln_g[0, 0]), row(ln_b[0, 0]),
             t_all, n, bsz, h_tail=ctx.reshape(bsz * m, D_MODEL))
    win, wuq, wabs, wuvt = _even_weights(ev_w_in[0], ev_w_uq[0], ev_w_ukv[0])
    tabs = _rope_tables(n, A_ROPE) + _rope_tables(n, B_HD)
    qa, ka, ct, qb, kb, vbt = _even_proj(h, mod[0, :, 3:6], win, row(ev_g_qlat[0]), row(ev_g_kvlat[0]), wuq, wabs,
                                         tabs, tiles_all, mod_index, tab_index)
    lam_init = 0.8 - 0.6 * math.exp(-0.3 * 0)
    mix_a = _mla(qa, ka, ct, wuvt, bsz, n, m)
    mix_b = _diff(qb, kb, vbt, ev_lam[0], ev_g_sub[0].reshape(2 * B_HD, 1), lam_init, bsz, n, m)
    h = _ffn(h, mod[0, :, 3:9], wg[0, 1], wu[0, 1], wd[0, 1], row(ln_g[0, 2]), row(ln_b[0, 2]), t_all, n, bsz,
             mixer=("tt", mix_a, mix_b, ev_w_out[0].astype(BF16), row(ln_g[0, 1]), row(ln_b[0, 1])))

    h = _ffn(h, mod[1, :, 0:3], wg[1, 0], wu[1, 0], wd[1, 0], row(ln_g[1, 0]), row(ln_b[1, 0]), t_all, n, bsz)
    u, qn, kn, vnt = _odd_proj(h, mod[1, :, 3:6], od_w_in[0].astype(BF16), tiles_all, mod_index)
    y_pool = _pool(u, od_w_pool[0].astype(BF16), row(od_pool_scale[0]), tiles_lat, tiles_seq)
    bias = _na_bias(od_rpb[0], n // GRID_W)
    y_na = _natten(qn, kn, vnt, bias, bsz, n, m)
    h = _ffn(h, mod[1, :, 3:9], wg[1, 1], wu[1, 1], wd[1, 1], row(ln_g[1, 2]), row(ln_b[1, 2]), t_lat, n, bsz,
             mixer=("nt", y_pool, y_na, od_w_out[0].astype(BF16), row(ln_g[1, 1]), row(ln_b[1, 1])))
    return h.reshape(bsz, n, D_MODEL)
```
